```python
import math
import functools
import jax
import jax.numpy as jnp
from jax import lax
import numpy as np

D_MODEL = 4096
BATCH = 4
SEQ = 2048
DEPTH = 1
DEC_BATCH = 128
DEC_SEQ = 1
PAST_LEN = 2048
PAGE_SIZE = 128

MIX_WIDTH = D_MODEL
ATTN_WIDTH = MIX_WIDTH // 2
CONV_WIDTH = MIX_WIDTH - ATTN_WIDTH
ATTN_HEAD_DIM = 128
N_HEADS = ATTN_WIDTH // ATTN_HEAD_DIM
QK_DIM = ATTN_HEAD_DIM // 2
ROPE_DIM = QK_DIM // 4
ROPE_THETA = 500000.0
ATTN_SCALE = QK_DIM ** -0.5
Q_BLOCK = 128
CONV_K = 3
N_IN = 3 * ATTN_WIDTH + 3 * CONV_WIDTH
IN_SPLITS = [ATTN_WIDTH, 2 * ATTN_WIDTH, 3 * ATTN_WIDTH,
             3 * ATTN_WIDTH + CONV_WIDTH, 3 * ATTN_WIDTH + 2 * CONV_WIDTH]
N_MEM = 256
N_CROSS_HEADS = 4
CROSS_HEAD_DIM = 256
CROSS_WIDTH = N_CROSS_HEADS * CROSS_HEAD_DIM
N_GROUPS = 8
EXPERTS_PER_GROUP = 8
N_EXPERTS = N_GROUPS * EXPERTS_PER_GROUP
TOP_K = 2
D_EXPERT = 512
MOE_BLOCK = 64
EPS = 1e-6

kernel_name = 'hymba_diffattn_shortconv_hmoe_step'


def _rmsnorm(x, g):
    xf = x.astype(jnp.float32)
    y = xf * lax.rsqrt(jnp.mean(xf * xf, axis=-1, keepdims=True) + EPS)
    return (y * g.astype(jnp.float32)).astype(x.dtype)


def _partial_rope(x, pos):
    half = ROPE_DIM // 2
    inv_freq = ROPE_THETA ** (-jnp.arange(half, dtype=jnp.float32) * (2.0 / ROPE_DIM))
    ang = pos.astype(jnp.float32)[:, None] * inv_freq[None, :]
    cos = jnp.cos(ang)[:, None, :]
    sin = jnp.sin(ang)[:, None, :]
    xr = x[..., :ROPE_DIM].astype(jnp.float32)
    x1, x2 = xr[..., :half], xr[..., half:]
    rot = jnp.concatenate([x1 * cos - x2 * sin, x2 * cos + x1 * sin], axis=-1).astype(x.dtype)
    return jnp.concatenate([rot, x[..., ROPE_DIM:]], axis=-1)


def _pair(t):
    return jnp.stack([t[..., :QK_DIM], t[..., QK_DIM:]])


def _diff_lambda(lq1, lk1, lq2, lk2, lam_init):
    f32 = jnp.float32
    return (jnp.exp(jnp.sum(lq1.astype(f32) * lk1.astype(f32)))
            - jnp.exp(jnp.sum(lq2.astype(f32) * lk2.astype(f32))) + lam_init)


def _diff_attn_causal(q, k, v, lam):
    S = q.shape[1]
    q12, k12 = _pair(q), _pair(k)
    vf = v.astype(jnp.float32)
    outs = []
    for blk in range(S // Q_BLOCK):
        start, end = blk * Q_BLOCK, (blk + 1) * Q_BLOCK
        s = jnp.einsum('mbqhd,mbkhd->mbhqk', q12[:, :, start:end], k12[:, :, :end],
                       preferred_element_type=jnp.float32) * ATTN_SCALE
        causal = jnp.arange(end)[None, :] <= jnp.arange(start, end)[:, None]
        p = jax.nn.softmax(jnp.where(causal, s, -jnp.inf), axis=-1)
        o = jnp.einsum('mbhqk,bkhd->mbqhd', p, vf[:, :end])
        outs.append(o[0] - lam * o[1])
    return jnp.concatenate(outs, axis=1)


def _online_softmax_update(carry, s, v):
    m, l, acc = carry
    m_new = jnp.maximum(m, jnp.max(s, axis=-1))
    corr = jnp.exp(m - m_new)
    p = jnp.exp(s - m_new[..., None])
    acc = acc * corr[..., None] + jnp.einsum('mbhqk,bkhd->mbhqd', p, v.astype(jnp.float32))
    return (m_new, l * corr + jnp.sum(p, axis=-1), acc)


def _diff_attn_paged(q, k, v, lam, cache_k, cache_v, page_table, layer):
    Bd, Sd = q.shape[:2]
    q12 = _pair(q)

    def page_step(carry, pages):
        kp = cache_k[layer, pages]
        vp = cache_v[layer, pages]
        s = jnp.einsum('mbqhd,mbkhd->mbhqk', q12, _pair(kp),
                       preferred_element_type=jnp.float32) * ATTN_SCALE
        return _online_softmax_update(carry, s, vp), None

    init = (jnp.full((2, Bd, N_HEADS, Sd), -jnp.inf, jnp.float32),
            jnp.zeros((2, Bd, N_HEADS, Sd), jnp.float32),
            jnp.zeros((2, Bd, N_HEADS, Sd, ATTN_HEAD_DIM), jnp.float32))
    carry, _ = lax.scan(page_step, init, page_table.T)
    s = jnp.einsum('mbqhd,mbkhd->mbhqk', q12, _pair(k),
                   preferred_element_type=jnp.float32) * ATTN_SCALE
    causal = jnp.arange(Sd)[None, :] <= jnp.arange(Sd)[:, None]
    m, l, acc = _online_softmax_update(carry, jnp.where(causal, s, -jnp.inf), v)
    o = acc / l[..., None]
    return jnp.transpose(o[0] - lam * o[1], (0, 2, 1, 3))


def _short_conv(z, state, w):
    S = z.shape[1]
    zp = jnp.concatenate([state.astype(z.dtype), z], axis=1)
    y = w[0] * zp[:, 0:S]
    for j in range(1, CONV_K):
        y = y + w[j] * zp[:, j:j + S]
    return y, zp[:, -(CONV_K - 1):]


def _mixer(a, pos, conv_state, attend, lam, lam_init, w_in, w_conv, g_subln, w_out):
    B, S, _ = a.shape
    u = a @ w_in
    q, k, v, cb, cc, cx = jnp.split(u, IN_SPLITS, axis=-1)
    q = _partial_rope(q.reshape(B, S, 2 * N_HEADS, QK_DIM), pos).reshape(B, S, N_HEADS, 2 * QK_DIM)
    k = _partial_rope(k.reshape(B, S, 2 * N_HEADS, QK_DIM), pos).reshape(B, S, N_HEADS, 2 * QK_DIM)
    v = v.reshape(B, S, N_HEADS, ATTN_HEAD_DIM)
    o = attend(q, k, v, lam)
    o = (_rmsnorm(o, g_subln) * (1.0 - lam_init)).astype(a.dtype).reshape(B, S, ATTN_WIDTH)
    yc, new_conv = _short_conv(cc * cx, conv_state, w_conv)
    y = jnp.concatenate([o, cb * yc], axis=-1) @ w_out
    return y, k, v, new_conv


def _mem_kv(mem, g_mem, w_ck, w_cv):
    B = mem.shape[0]
    m = _rmsnorm(mem, g_mem)
    k = (m @ w_ck).reshape(B, N_MEM, N_CROSS_HEADS, CROSS_HEAD_DIM)
    v = (m @ w_cv).reshape(B, N_MEM, N_CROSS_HEADS, CROSS_HEAD_DIM)
    return k, v


def _cross_attend(h, mem_k, mem_v, g_cross, w_cq, w_co):
    B, S, _ = h.shape
    q = (_rmsnorm(h, g_cross) @ w_cq).reshape(B, S, N_CROSS_HEADS, CROSS_HEAD_DIM)
    s = jnp.einsum('bqhd,bkhd->bhqk', q, mem_k, preferred_element_type=jnp.float32) * (CROSS_HEAD_DIM ** -0.5)
    p = jax.nn.softmax(s, axis=-1)
    o = jnp.einsum('bhqk,bkhd->bqhd', p, mem_v.astype(jnp.float32)).astype(h.dtype)
    return o.reshape(B, S, CROSS_WIDTH) @ w_co


def _hier_moe(x2d, w_rg, b_rg, w_re, b_re, w_e_gate, w_e_up, w_e_down, layer):
    f32 = jnp.float32
    T, D = x2d.shape
    xf = x2d.astype(f32)
    g_prob = jax.nn.softmax(xf @ w_rg.astype(f32) + b_rg.astype(f32), axis=-1)
    g_top_p, g_top_i = lax.top_k(g_prob, 1)
    gsel = g_top_i[:, 0]
    e_logits = (xf @ w_re.astype(f32) + b_re.astype(f32)).reshape(T, N_GROUPS, EXPERTS_PER_GROUP)
    e_logits = jnp.take_along_axis(e_logits, gsel[:, None, None], axis=1)[:, 0]
    e_top_p, e_top_i = lax.top_k(jax.nn.softmax(e_logits, axis=-1), TOP_K)
    gate = g_top_p * (e_top_p / jnp.sum(e_top_p, axis=-1, keepdims=True))
    expert = gsel[:, None] * EXPERTS_PER_GROUP + e_top_i

    n_assign = T * TOP_K
    e_flat = expert.reshape(n_assign)
    tok_flat = jnp.repeat(jnp.arange(T, dtype=jnp.int32), TOP_K)
    gate_flat = gate.reshape(n_assign)
    order = jnp.argsort(e_flat)
    e_sorted = e_flat[order]
    counts = jnp.bincount(e_flat, length=N_EXPERTS)
    starts = jnp.cumsum(counts) - counts
    padded = (counts + MOE_BLOCK - 1) // MOE_BLOCK * MOE_BLOCK
    pad_end = jnp.cumsum(padded)
    slot = pad_end[e_sorted] - padded[e_sorted] + jnp.arange(n_assign) - starts[e_sorted]
    n_blocks = (n_assign + N_EXPERTS * (MOE_BLOCK - 1) + MOE_BLOCK - 1) // MOE_BLOCK
    n_slots = n_blocks * MOE_BLOCK
    buf_tok = jnp.full((n_slots,), T, jnp.int32).at[slot].set(tok_flat[order])
    buf_gate = jnp.zeros((n_slots,), f32).at[slot].set(gate_flat[order])
    block_expert = jnp.minimum(
        jnp.searchsorted(pad_end, jnp.arange(n_blocks) * MOE_BLOCK, side='right'), N_EXPERTS - 1)
    x_pad = jnp.concatenate([x2d, jnp.zeros((1, D), x2d.dtype)], axis=0)

    def expert_block(args):
        tok, e = args
        xb = x_pad[tok]
        hb = jax.nn.silu(xb @ w_e_gate[layer, e]) * (xb @ w_e_up[layer, e])
        return hb @ w_e_down[layer, e]

    out = lax.map(expert_block, (buf_tok.reshape(n_blocks, MOE_BLOCK), block_expert))
    y = jax.ops.segment_sum(out.reshape(n_slots, D).astype(f32) * buf_gate[:, None], buf_tok,
                            num_segments=T + 1)
    return y[:T].astype(x2d.dtype)


def _layer(h, pos, conv_state, mem_k, mem_v, attend, lam, lam_init, layer,
           g_mix, w_in, w_conv, g_subln, w_out, g_cross, w_cq, w_co, g_ffn,
           w_rg, b_rg, w_re, b_re, w_e_gate, w_e_up, w_e_down):
    y, k, v, new_conv = _mixer(_rmsnorm(h, g_mix), pos, conv_state, attend, lam, lam_init,
                               w_in, w_conv, g_subln, w_out)
    h = h + y
    h = h + _cross_attend(h, mem_k, mem_v, g_cross, w_cq, w_co)
    B, S, D = h.shape
    h = h + _hier_moe(_rmsnorm(h, g_ffn).reshape(B * S, D), w_rg, b_rg, w_re, b_re,
                      w_e_gate, w_e_up, w_e_down, layer).reshape(B, S, D)
    return h, k, v, new_conv


def setup_inputs(seed: int = 0) -> dict:
    key = jax.random.key(seed)
    ks = jax.random.split(key, 40)
    f32 = jnp.float32

    def nrm(k, shape, scale=1.0):
        return jax.random.normal(k, shape, f32) * scale

    def gain(k, shape):
        return 1.0 + 0.02 * jax.random.normal(k, shape, f32)

    n_pages = PAST_LEN // PAGE_SIZE
    n_used = DEC_BATCH * n_pages
    n_pool = n_used + max(1, n_used // 4)
    page_table = jax.random.permutation(ks[0], n_pool)[:n_used].reshape(DEC_BATCH, n_pages).astype(jnp.int32)
    return {
        'x_prompt': nrm(ks[1], (BATCH, SEQ, D_MODEL)),
        'x_sample': nrm(ks[2], (DEC_BATCH, DEC_SEQ, D_MODEL)),
        'cache_k': nrm(ks[3], (DEPTH, n_pool, PAGE_SIZE, N_HEADS, 2 * QK_DIM)),
        'cache_v': nrm(ks[4], (DEPTH, n_pool, PAGE_SIZE, N_HEADS, ATTN_HEAD_DIM)),
        'state_conv': nrm(ks[5], (DEPTH, DEC_BATCH, CONV_K - 1, CONV_WIDTH)),
        'cache_mem_k': nrm(ks[6], (DEPTH, DEC_BATCH, N_MEM, N_CROSS_HEADS, CROSS_HEAD_DIM)),
        'cache_mem_v': nrm(ks[7], (DEPTH, DEC_BATCH, N_MEM, N_CROSS_HEADS, CROSS_HEAD_DIM)),
        'page_table': page_table,
        'mem_prompt': nrm(ks[8], (BATCH, N_MEM, D_MODEL)),
        'g_mix': gain(ks[9], (DEPTH, D_MODEL)),
        'w_in': nrm(ks[10], (DEPTH, D_MODEL, N_IN), D_MODEL ** -0.5),
        'lambda_q1': nrm(ks[11], (DEPTH, QK_DIM), 0.1),
        'lambda_k1': nrm(ks[12], (DEPTH, QK_DIM), 0.1),
        'lambda_q2': nrm(ks[13], (DEPTH, QK_DIM), 0.1),
        'lambda_k2': nrm(ks[14], (DEPTH, QK_DIM), 0.1),
        'g_subln': gain(ks[15], (DEPTH, ATTN_HEAD_DIM)),
        'w_conv': nrm(ks[16], (DEPTH, CONV_K, CONV_WIDTH), CONV_K ** -0.5),
        'w_out': nrm(ks[17], (DEPTH, MIX_WIDTH, D_MODEL), MIX_WIDTH ** -0.5),
        'g_mem': gain(ks[18], (DEPTH, D_MODEL)),
        'w_ck': nrm(ks[19], (DEPTH, D_MODEL, CROSS_WIDTH), D_MODEL ** -0.5),
        'w_cv': nrm(ks[20], (DEPTH, D_MODEL, CROSS_WIDTH), D_MODEL ** -0.5),
        'g_cross': gain(ks[21], (DEPTH, D_MODEL)),
        'w_cq': nrm(ks[22], (DEPTH, D_MODEL, CROSS_WIDTH), D_MODEL ** -0.5),
        'w_co': nrm(ks[23], (DEPTH, CROSS_WIDTH, D_MODEL), CROSS_WIDTH ** -0.5),
        'g_ffn': gain(ks[24], (DEPTH, D_MODEL)),
        'w_router_group': nrm(ks[25], (DEPTH, D_MODEL, N_GROUPS), D_MODEL ** -0.5),
        'b_router_group': nrm(ks[26], (DEPTH, N_GROUPS), 0.01),
        'w_router_expert': nrm(ks[27], (DEPTH, D_MODEL, N_EXPERTS), D_MODEL ** -0.5),
        'b_router_expert': nrm(ks[28], (DEPTH, N_EXPERTS), 0.01),
        'w_e_gate': nrm(ks[29], (DEPTH, N_EXPERTS, D_MODEL, D_EXPERT), D_MODEL ** -0.5),
        'w_e_up': nrm(ks[30], (DEPTH, N_EXPERTS, D_MODEL, D_EXPERT), D_MODEL ** -0.5),
        'w_e_down': nrm(ks[31], (DEPTH, N_EXPERTS, D_EXPERT, D_MODEL), D_EXPERT ** -0.5),
        'g_final': gain(ks[32], (D_MODEL,)),
    }


def reference(x_prompt, x_sample, cache_k, cache_v, state_conv, cache_mem_k, cache_mem_v, page_table,
              mem_prompt, g_mix, w_in, lambda_q1, lambda_k1, lambda_q2, lambda_k2, g_subln, w_conv, w_out,
              g_mem, w_ck, w_cv, g_cross, w_cq, w_co, g_ffn, w_router_group, b_router_group,
              w_router_expert, b_router_expert, w_e_gate, w_e_up, w_e_down, g_final):
    seq = x_prompt.shape[1]
    dec_seq = x_sample.shape[1]
    past_len = page_table.shape[1] * cache_k.shape[2]
    pos_p = jnp.arange(seq, dtype=jnp.int32)
    pos_s = past_len + jnp.arange(dec_seq, dtype=jnp.int32)
    conv0 = jnp.zeros((x_prompt.shape[0], CONV_K - 1, CONV_WIDTH), x_prompt.dtype)

    hp, hs = x_prompt, x_sample
    k_p, v_p, c_p, mk_p, mv_p, k_s, v_s, c_s = [], [], [], [], [], [], [], []
    for l in range(DEPTH):
        lam_init = 0.8 - 0.6 * math.exp(-0.3 * l)
        lam = _diff_lambda(lambda_q1[l], lambda_k1[l], lambda_q2[l], lambda_k2[l], lam_init)
        mem_k, mem_v = _mem_kv(mem_prompt, g_mem[l], w_ck[l], w_cv[l])
        attend_s = functools.partial(_diff_attn_paged, cache_k=cache_k, cache_v=cache_v,
                                     page_table=page_table, layer=l)
        hp, kp, vp, cp = _layer(hp, pos_p, conv0, mem_k, mem_v, _diff_attn_causal, lam, lam_init, l,
                                g_mix[l], w_in[l], w_conv[l], g_subln[l], w_out[l],
                                g_cross[l], w_cq[l], w_co[l], g_ffn[l],
                                w_router_group[l], b_router_group[l], w_router_expert[l], b_router_expert[l],
                                w_e_gate, w_e_up, w_e_down)
        hs, ks_, vs_, cs_ = _layer(hs, pos_s, state_conv[l], cache_mem_k[l], cache_mem_v[l], attend_s,
                                   lam, lam_init, l,
                                   g_mix[l], w_in[l], w_conv[l], g_subln[l], w_out[l],
                                   g_cross[l], w_cq[l], w_co[l], g_ffn[l],
                                   w_router_group[l], b_router_group[l], w_router_expert[l], b_router_expert[l],
                                   w_e_gate, w_e_up, w_e_down)
        k_p.append(kp)
        v_p.append(vp)
        c_p.append(cp)
        mk_p.append(mem_k)
        mv_p.append(mem_v)
        k_s.append(ks_)
        v_s.append(vs_)
        c_s.append(cs_)

    y_prompt = _rmsnorm(hp, g_final)
    y_sample = _rmsnorm(hs, g_final)
    return (y_prompt, y_sample, jnp.stack(k_p), jnp.stack(v_p), jnp.stack(c_p), jnp.stack(mk_p),
            jnp.stack(mv_p), jnp.stack(k_s), jnp.stack(v_s), jnp.stack(c_s))
```

```python
import functools
import math

import jax
import jax.numpy as jnp
from jax import lax
from jax.experimental import pallas as pl
from jax.experimental.pallas import tpu as pltpu

F32 = jnp.float32
BF16 = jnp.bfloat16

EPS = 1e-6
HEAD_DIM = 128
QK_DIM = HEAD_DIM // 2
ROPE_DIM = QK_DIM // 4
ROPE_HALF = ROPE_DIM // 2
ROPE_THETA = 500000.0
N_CROSS_HEADS = 4
N_GROUPS = 8
EXPERTS_PER_GROUP = 8
N_EXPERTS = N_GROUPS * EXPERTS_PER_GROUP
CONV_K = 3

LANES = 128
V7X_VMEM_BYTES = 64 * 1024 * 1024
VMEM_CAP = V7X_VMEM_BYTES - 6 * 1024 * 1024
MOE_ROWS = 256
MOE_F_SPLIT = 2
COMBINE_ROWS = 128


def _pick(n, pref):
    if n <= pref:
        return n
    for d in range(pref, 7, -1):
        if n % d == 0 and d % 8 == 0:
            return d
    return n


def _params(sem, vmem_est):
    limit = int(min(VMEM_CAP, max(32 * 1024 * 1024, vmem_est * 5 // 4 + (2 << 20))))
    return pltpu.CompilerParams(dimension_semantics=sem, vmem_limit_bytes=limit)


def _nt_dot(a, b):
    return lax.dot_general(a, b, (((1,), (1,)), ((), ())), preferred_element_type=F32)


def _rmsnorm_kernel(x_ref, g_ref, o_ref):
    x = x_ref[...]
    y = x * lax.rsqrt(jnp.mean(x * x, axis=-1, keepdims=True) + EPS)
    o_ref[...] = (y * g_ref[...]).astype(o_ref.dtype)


def _rmsnorm_bf16(x, g):
    m, d = x.shape
    bm = _pick(m, 256)
    return pl.pallas_call(
        _rmsnorm_kernel,
        out_shape=jax.ShapeDtypeStruct((m, d), BF16),
        grid=(m // bm,),
        in_specs=[pl.BlockSpec((bm, d), lambda i: (i, 0)),
                  pl.BlockSpec((1, d), lambda i: (0, 0))],
        out_specs=pl.BlockSpec((bm, d), lambda i: (i, 0)),
        compiler_params=_params(("parallel",), 2 * bm * d * 6),
        name="rmsnorm_bf16",
    )(x, g.reshape(1, d))


def _mm_kernel(*refs, k_splits, mode, scale):
    n_lhs = len(k_splits)
    lhs = refs[:n_lhs]
    w_ref = refs[n_lhs]
    extra = refs[n_lhs + 1:-1]
    o_ref = refs[-1]
    acc = None
    k0 = 0
    for a_ref, kk in zip(lhs, k_splits):
        part = jnp.dot(a_ref[...], w_ref[k0:k0 + kk, :], preferred_element_type=F32)
        acc = part if acc is None else acc + part
        k0 += kk
    if mode == "rope":
        cos_ref, sa_ref, sb_ref = extra
        cos, sa, sb = cos_ref[...], sa_ref[...], sb_ref[...]
        for c in range(acc.shape[1] // LANES):
            blk = acc[:, c * LANES:(c + 1) * LANES]
            up = pltpu.roll(blk, LANES - ROPE_HALF, axis=1)
            dn = pltpu.roll(blk, ROPE_HALF, axis=1)
            rot = blk * cos + up * sa + dn * sb
            if scale != 1.0:
                rot = rot * scale
            o_ref[:, c * LANES:(c + 1) * LANES] = rot.astype(o_ref.dtype)
        return
    if mode == "residual":
        acc = extra[0][...] + acc
    if scale != 1.0:
        acc = acc * scale
    o_ref[...] = acc.astype(o_ref.dtype)


def _matmul(lhs_list, w, col0, n, *, out_dtype, mode="plain", scale=1.0, rope=None,
            rope_period=None, residual=None, bm_pref=1024, bn_pref=1024, name="matmul"):
    m = lhs_list[0].shape[0]
    k_splits = tuple(a.shape[1] for a in lhs_list)
    k_total = sum(k_splits)
    assert w.shape[0] == k_total
    bm = _pick(m, bm_pref)
    bn = _pick(n, bn_pref)
    assert col0 % bn == 0 and n % bn == 0 and m % bm == 0
    joff = col0 // bn
    in_specs = [pl.BlockSpec((bm, kk), lambda i, j: (i, 0)) for kk in k_splits]
    in_specs.append(pl.BlockSpec((k_total, bn), lambda i, j: (0, j + joff)))
    args = list(lhs_list) + [w]
    vmem = 2 * (bm * k_total * 2 + k_total * bn * 2 + bm * bn * 4) + 2 * bm * bn * 4
    if mode == "rope":
        assert rope_period % bm == 0 or bm % rope_period == 0
        if bm > rope_period:
            rope = tuple(jnp.tile(t, (bm // rope_period, 1)) for t in rope)
            nper = 1
        else:
            nper = rope_period // bm
        in_specs += [pl.BlockSpec((bm, LANES), lambda i, j: (i % nper, 0))] * 3
        args += list(rope)
    elif mode == "residual":
        in_specs.append(pl.BlockSpec((bm, bn), lambda i, j: (i, j)))
        args.append(residual)
        vmem += 2 * bm * bn * 4
    return pl.pallas_call(
        functools.partial(_mm_kernel, k_splits=k_splits, mode=mode, scale=scale),
        out_shape=jax.ShapeDtypeStruct((m, n), out_dtype),
        grid=(m // bm, n // bn),
        in_specs=in_specs,
        out_specs=pl.BlockSpec((bm, bn), lambda i, j: (i, j)),
        compiler_params=_params(("parallel", "parallel"), vmem),
        name=name,
    )(*args)


def _rope_tables(pos):
    inv_freq = ROPE_THETA ** (-jnp.arange(ROPE_HALF, dtype=F32) * (2.0 / ROPE_DIM))
    ang = pos.astype(F32)[:, None] * inv_freq[None, :]
    cos, sin = jnp.cos(ang), jnp.sin(ang)
    n = pos.shape[0]
    one = jnp.ones((n, QK_DIM - ROPE_DIM), F32)
    zero = jnp.zeros((n, QK_DIM - ROPE_DIM), F32)
    z8 = jnp.zeros((n, ROPE_HALF), F32)
    cos64 = jnp.concatenate([cos, cos, one], axis=1)
    up64 = jnp.concatenate([-sin, z8, zero], axis=1)
    dn64 = jnp.concatenate([z8, sin, zero], axis=1)
    rep = LANES // QK_DIM
    return tuple(jnp.tile(t, (1, rep)) for t in (cos64, up64, dn64))


def _conv_seq_kernel(a_ref, wb_ref, wc_ref, wx_ref, wconv_ref, g_ref, nc_ref, zbuf, *, tiles_per_seq):
    i = pl.program_id(1)
    bm = a_ref.shape[0]
    a = a_ref[...]
    cb = jnp.dot(a, wb_ref[...], preferred_element_type=F32)
    cc = jnp.dot(a, wc_ref[...], preferred_element_type=F32)
    cx = jnp.dot(a, wx_ref[...], preferred_element_type=F32)
    z = cc * cx
    first = (i % tiles_per_seq) == 0

    @pl.when(first)
    def _():
        zbuf[0:8, :] = jnp.zeros((8, zbuf.shape[1]), F32)

    @pl.when(jnp.logical_not(first))
    def _():
        zbuf[0:8, :] = zbuf[bm:bm + 8, :]

    zbuf[8:8 + bm, :] = z
    w = wconv_ref[...]
    y = w[0:1, :] * zbuf[6:6 + bm, :] + w[1:2, :] * zbuf[7:7 + bm, :]
    y = y + w[2:3, :] * z
    g_ref[...] = (cb * y).astype(g_ref.dtype)

    @pl.when((i % tiles_per_seq) == tiles_per_seq - 1)
    def _():
        nc_ref[0] = z[bm - (CONV_K - 1):, :]


def _conv_branch_seq(a, w, col_b, col_c, col_x, w_conv, batch, seq):
    m, k = a.shape
    cw = w_conv.shape[1]
    assert seq >= 8
    bm = _pick(seq, 512)
    bc = _pick(cw, 512)
    tps = seq // bm
    ob, oc, ox = col_b // bc, col_c // bc, col_x // bc
    vmem = 2 * (bm * k * 2 + 3 * k * bc * 2 + bm * bc * 2) + 6 * bm * bc * 4
    return pl.pallas_call(
        functools.partial(_conv_seq_kernel, tiles_per_seq=tps),
        out_shape=(jax.ShapeDtypeStruct((m, cw), BF16),
                   jax.ShapeDtypeStruct((batch, CONV_K - 1, cw), F32)),
        grid=(cw // bc, m // bm),
        in_specs=[pl.BlockSpec((bm, k), lambda c, i: (i, 0)),
                  pl.BlockSpec((k, bc), lambda c, i: (0, ob + c)),
                  pl.BlockSpec((k, bc), lambda c, i: (0, oc + c)),
                  pl.BlockSpec((k, bc), lambda c, i: (0, ox + c)),
                  pl.BlockSpec((CONV_K, bc), lambda c, i: (0, c))],
        out_specs=(pl.BlockSpec((bm, bc), lambda c, i: (i, c)),
                   pl.BlockSpec((1, CONV_K - 1, bc), lambda c, i: (i // tps, 0, c))),
        scratch_shapes=[pltpu.VMEM((bm + 8, bc), F32)],
        compiler_params=_params(("parallel", "arbitrary"), vmem),
        name="conv_branch_seq",
    )(a, w, w, w, w_conv)


def _conv_step_kernel(a_ref, wb_ref, wc_ref, wx_ref, wconv_ref, s0_ref, s1_ref, g_ref, z_ref):
    a = a_ref[...]
    cb = jnp.dot(a, wb_ref[...], preferred_element_type=F32)
    cc = jnp.dot(a, wc_ref[...], preferred_element_type=F32)
    cx = jnp.dot(a, wx_ref[...], preferred_element_type=F32)
    z = cc * cx
    w = wconv_ref[...]
    y = w[0:1, :] * s0_ref[...] + w[1:2, :] * s1_ref[...]
    y = y + w[2:3, :] * z
    g_ref[...] = (cb * y).astype(g_ref.dtype)
    z_ref[...] = z


def _conv_branch_step(a, w, col_b, col_c, col_x, w_conv, s0, s1):
    m, k = a.shape
    cw = w_conv.shape[1]
    bc = _pick(cw, 512)
    ob, oc, ox = col_b // bc, col_c // bc, col_x // bc
    vmem = 2 * (m * k * 2 + 3 * k * bc * 2 + 5 * m * bc * 4)
    return pl.pallas_call(
        _conv_step_kernel,
        out_shape=(jax.ShapeDtypeStruct((m, cw), BF16), jax.ShapeDtypeStruct((m, cw), F32)),
        grid=(cw // bc,),
        in_specs=[pl.BlockSpec((m, k), lambda c: (0, 0)),
                  pl.BlockSpec((k, bc), lambda c: (0, ob + c)),
                  pl.BlockSpec((k, bc), lambda c: (0, oc + c)),
                  pl.BlockSpec((k, bc), lambda c: (0, ox + c)),
                  pl.BlockSpec((CONV_K, bc), lambda c: (0, c)),
                  pl.BlockSpec((m, bc), lambda c: (0, c)),
                  pl.BlockSpec((m, bc), lambda c: (0, c))],
        out_specs=(pl.BlockSpec((m, bc), lambda c: (0, c)),
                   pl.BlockSpec((m, bc), lambda c: (0, c))),
        compiler_params=_params(("parallel",), vmem),
        name="conv_branch_step",
    )(a, w, w, w, w_conv, s0, s1)


def _diff_lambda(lq1_ref, lk1_ref, lq2_ref, lk2_ref, lam_init):
    a = jnp.sum(lq1_ref[...] * lk1_ref[...], axis=-1, keepdims=True)
    b = jnp.sum(lq2_ref[...] * lk2_ref[...], axis=-1, keepdims=True)
    return jnp.exp(a) - jnp.exp(b) + lam_init


def _subln(d, gsub, lam_init):
    dn = d * lax.rsqrt(jnp.mean(d * d, axis=-1, keepdims=True) + EPS)
    return dn * gsub * (1.0 - lam_init)


def _causal_attn_kernel(q_ref, k_ref, v_ref, lq1, lk1, lq2, lk2, gsub_ref, o_ref, kb, vb, *, bq, lam_init):
    seq = q_ref.shape[1]
    kb[...] = k_ref[0].astype(BF16)
    vb[...] = v_ref[0].astype(BF16)
    lam = _diff_lambda(lq1, lk1, lq2, lk2, lam_init)
    gsub = gsub_ref[...]
    lane = lax.broadcasted_iota(jnp.int32, (bq, HEAD_DIM), 1)
    row = lax.broadcasted_iota(jnp.int32, (2 * bq, bq), 0)
    col = lax.broadcasted_iota(jnp.int32, (2 * bq, bq), 1)
    tri = col <= jnp.where(row >= bq, row - bq, row)
    for qi in range(seq // bq):
        q = q_ref[0, qi * bq:(qi + 1) * bq, :]
        zero = jnp.zeros_like(q)
        qs = jnp.concatenate([jnp.where(lane < QK_DIM, q, zero), jnp.where(lane >= QK_DIM, q, zero)], axis=0)
        past = qi * bq
        s_diag = jnp.where(tri, _nt_dot(qs, kb[past:past + bq, :]), -jnp.inf)
        m = jnp.max(s_diag, axis=-1, keepdims=True)
        if past:
            s_past = _nt_dot(qs, kb[0:past, :])
            m = jnp.maximum(m, jnp.max(s_past, axis=-1, keepdims=True))
        p_diag = jnp.exp(s_diag - m)
        l = jnp.sum(p_diag, axis=-1, keepdims=True)
        o = jnp.dot(p_diag.astype(BF16), vb[past:past + bq, :], preferred_element_type=F32)
        if past:
            p_past = jnp.exp(s_past - m)
            l = l + jnp.sum(p_past, axis=-1, keepdims=True)
            o = o + jnp.dot(p_past.astype(BF16), vb[0:past, :], preferred_element_type=F32)
        o = o / l
        d = o[:bq] - lam * o[bq:]
        o_ref[0, qi * bq:(qi + 1) * bq, :] = _subln(d, gsub, lam_init).astype(o_ref.dtype)


def _causal_diff_attn(q, k, v, lams, gsub, lam_init):
    b, s, w = q.shape
    h = w // HEAD_DIM
    bq = _pick(s, 256)
    blk = lambda: pl.BlockSpec((1, s, HEAD_DIM), lambda bi, hi: (bi, 0, hi))
    small = lambda n: pl.BlockSpec((1, n), lambda bi, hi: (0, 0))
    vmem = 2 * s * HEAD_DIM * (2 + 4 + 4 + 2) + 2 * s * HEAD_DIM * 2 + 6 * 2 * bq * s * 4
    return pl.pallas_call(
        functools.partial(_causal_attn_kernel, bq=bq, lam_init=lam_init),
        out_shape=jax.ShapeDtypeStruct((b, s, w), BF16),
        grid=(b, h),
        in_specs=[blk(), blk(), blk(), small(QK_DIM), small(QK_DIM), small(QK_DIM), small(QK_DIM),
                  small(HEAD_DIM)],
        out_specs=blk(),
        scratch_shapes=[pltpu.VMEM((s, HEAD_DIM), BF16), pltpu.VMEM((s, HEAD_DIM), BF16)],
        compiler_params=_params(("parallel", "parallel"), vmem),
        name="causal_diff_attn",
    )(q, k, v, *lams, gsub)


def _paged_attn_kernel(pt_ref, q_ref, kc_ref, vc_ref, kp_ref, vp_ref, lq1, lk1, lq2, lk2, gsub_ref,
                       o_ref, qbd, m_s, l_s, acc, *, n_heads, lam_init):
    p = pl.program_id(1)
    w = n_heads * HEAD_DIM
    nrow = 2 * n_heads
    rowi = lax.broadcasted_iota(jnp.int32, (nrow, w), 0)
    coli = lax.broadcasted_iota(jnp.int32, (nrow, w), 1)
    r_head = jnp.where(rowi >= n_heads, rowi - n_heads, rowi)
    r_map = jnp.where(rowi >= n_heads, 1, 0)
    same_head = (coli // HEAD_DIM) == r_head

    @pl.when(p == 0)
    def _():
        qrow = jnp.broadcast_to(q_ref[0].astype(F32), (nrow, w))
        sel = jnp.logical_and(same_head, ((coli // QK_DIM) % 2) == r_map)
        qbd[...] = jnp.where(sel, qrow, 0.0).astype(BF16)
        m_s[...] = jnp.full(m_s.shape, -jnp.inf, F32)
        l_s[...] = jnp.zeros(l_s.shape, F32)
        acc[...] = jnp.zeros(acc.shape, F32)

    s = _nt_dot(qbd[...], kp_ref[0].astype(BF16))
    m_old = m_s[...]
    m_new = jnp.maximum(m_old, jnp.max(s, axis=-1, keepdims=True))
    corr = jnp.exp(m_old - m_new)
    pr = jnp.exp(s - m_new)
    l_s[...] = l_s[...] * corr + jnp.sum(pr, axis=-1, keepdims=True)
    acc[...] = acc[...] * corr + jnp.dot(pr.astype(BF16), vp_ref[0].astype(BF16), preferred_element_type=F32)
    m_s[...] = m_new

    @pl.when(p == pl.num_programs(1) - 1)
    def _():
        kc = kc_ref[0].astype(BF16).astype(F32)
        vc = vc_ref[0].astype(BF16).astype(F32)
        s_c = jnp.sum(qbd[...].astype(F32) * kc, axis=-1, keepdims=True)
        m_old = m_s[...]
        m_new = jnp.maximum(m_old, s_c)
        corr = jnp.exp(m_old - m_new)
        p_c = jnp.exp(s_c - m_new)
        l = l_s[...] * corr + p_c
        a = acc[...] * corr + p_c.astype(BF16).astype(F32) * vc
        o = a / l
        lam = _diff_lambda(lq1, lk1, lq2, lk2, lam_init)
        d = o[:n_heads] - lam * o[n_heads:]
        dm = jnp.where(same_head[:n_heads], d, 0.0)
        ms = jnp.sum(dm * dm, axis=-1, keepdims=True) * (1.0 / HEAD_DIM)
        dn = dm * lax.rsqrt(ms + EPS) * gsub_ref[...] * (1.0 - lam_init)
        o_ref[0] = jnp.sum(dn, axis=0, keepdims=True).astype(o_ref.dtype)


def _paged_diff_attn(q, k_cur, v_cur, cache_k, cache_v, page_table, lams, gsub, lam_init):
    bd, w = q.shape
    n_heads = w // HEAD_DIM
    n_pages = page_table.shape[1]
    page = cache_k.shape[1]
    row = lambda: pl.BlockSpec((1, 1, w), lambda b, p, pt: (b, 0, 0))
    pg = lambda: pl.BlockSpec((1, page, w), lambda b, p, pt: (pt[b * n_pages + p], 0, 0))
    small = lambda n: pl.BlockSpec((1, n), lambda b, p, pt: (0, 0))
    gs = pltpu.PrefetchScalarGridSpec(
        num_scalar_prefetch=1,
        grid=(bd, n_pages),
        in_specs=[row(), row(), row(), pg(), pg(), small(QK_DIM), small(QK_DIM), small(QK_DIM),
                  small(QK_DIM), small(w)],
        out_specs=row(),
        scratch_shapes=[pltpu.VMEM((2 * n_heads, w), BF16), pltpu.VMEM((2 * n_heads, 1), F32),
                        pltpu.VMEM((2 * n_heads, 1), F32), pltpu.VMEM((2 * n_heads, w), F32)],
    )
    vmem = 4 * page * w * 4 + 4 * page * w * 2 + 8 * 2 * n_heads * w * 4
    out = pl.pallas_call(
        functools.partial(_paged_attn_kernel, n_heads=n_heads, lam_init=lam_init),
        out_shape=jax.ShapeDtypeStruct((bd, 1, w), BF16),
        grid_spec=gs,
        compiler_params=_params(("parallel", "arbitrary"), vmem),
        name="paged_diff_attn",
    )(page_table.reshape(-1), q.reshape(bd, 1, w), k_cur.reshape(bd, 1, w), v_cur.reshape(bd, 1, w),
      cache_k, cache_v, *lams, jnp.tile(gsub, (1, n_heads)))
    return out.reshape(bd, w)


def _cross_attn_kernel(q_ref, mk_ref, mv_ref, o_ref, mkb, mvb):
    @pl.when(pl.program_id(1) == 0)
    def _():
        mkb[...] = mk_ref[0].astype(BF16)
        mvb[...] = mv_ref[0].astype(BF16)

    hd = q_ref.shape[2] // N_CROSS_HEADS
    for h in range(N_CROSS_HEADS):
        sl = slice(h * hd, (h + 1) * hd)
        s = _nt_dot(q_ref[0, :, sl], mkb[:, sl])
        p = jnp.exp(s - jnp.max(s, axis=-1, keepdims=True))
        l = jnp.sum(p, axis=-1, keepdims=True)
        o = jnp.dot(p.astype(BF16), mvb[:, sl], preferred_element_type=F32) / l
        o_ref[0, :, sl] = o.astype(o_ref.dtype)


def _cross_attn_seq(q, mem_k, mem_v):
    b, s, cw = q.shape
    n_mem = mem_k.shape[1]
    bq = _pick(s, 512)
    vmem = 2 * (2 * bq * cw * 2 + 2 * n_mem * cw * 4) + 2 * n_mem * cw * 2 + 6 * bq * n_mem * 4
    return pl.pallas_call(
        _cross_attn_kernel,
        out_shape=jax.ShapeDtypeStruct((b, s, cw), BF16),
        grid=(b, s // bq),
        in_specs=[pl.BlockSpec((1, bq, cw), lambda bi, i: (bi, i, 0)),
                  pl.BlockSpec((1, n_mem, cw), lambda bi, i: (bi, 0, 0)),
                  pl.BlockSpec((1, n_mem, cw), lambda bi, i: (bi, 0, 0))],
        out_specs=pl.BlockSpec((1, bq, cw), lambda bi, i: (bi, i, 0)),
        scratch_shapes=[pltpu.VMEM((n_mem, cw), BF16), pltpu.VMEM((n_mem, cw), BF16)],
        compiler_params=_params(("parallel", "arbitrary"), vmem),
        name="cross_attn_seq",
    )(q, mem_k, mem_v)


def _cross_step_kernel(q_ref, mk_ref, mv_ref, o_ref):
    bb, _, cw = q_ref.shape
    hd = cw // N_CROSS_HEADS
    rowi = lax.broadcasted_iota(jnp.int32, (8, cw), 0)
    coli = lax.broadcasted_iota(jnp.int32, (8, cw), 1)
    own = (coli // hd) == rowi
    for j in range(bb):
        qrow = jnp.broadcast_to(q_ref[j].astype(F32), (8, cw))
        qbd = jnp.where(own, qrow, 0.0).astype(BF16)
        s = _nt_dot(qbd, mk_ref[j].astype(BF16))
        p = jnp.exp(s - jnp.max(s, axis=-1, keepdims=True))
        l = jnp.sum(p, axis=-1, keepdims=True)
        o = jnp.dot(p.astype(BF16), mv_ref[j].astype(BF16), preferred_element_type=F32) / l
        o_ref[j] = jnp.sum(jnp.where(own, o, 0.0), axis=0, keepdims=True).astype(o_ref.dtype)


def _cross_attn_step(q, mem_k, mem_v):
    bd, cw = q.shape
    n_mem = mem_k.shape[1]
    bb = 4 if bd % 4 == 0 else 1
    vmem = 2 * 2 * bb * n_mem * cw * 4 + 4 * n_mem * cw * 2
    out = pl.pallas_call(
        _cross_step_kernel,
        out_shape=jax.ShapeDtypeStruct((bd, 1, cw), BF16),
        grid=(bd // bb,),
        in_specs=[pl.BlockSpec((bb, 1, cw), lambda i: (i, 0, 0)),
                  pl.BlockSpec((bb, n_mem, cw), lambda i: (i, 0, 0)),
                  pl.BlockSpec((bb, n_mem, cw), lambda i: (i, 0, 0))],
        out_specs=pl.BlockSpec((bb, 1, cw), lambda i: (i, 0, 0)),
        compiler_params=_params(("parallel",), vmem),
        name="cross_attn_step",
    )(q.reshape(bd, 1, cw), mem_k, mem_v)
    return out.reshape(bd, cw)


def _router_kernel(h_ref, g_ref, wr_ref, br_ref, meta_ref):
    x = h_ref[...]
    xn = x * lax.rsqrt(jnp.mean(x * x, axis=-1, keepdims=True) + EPS) * g_ref[...]
    logits = jnp.dot(xn, wr_ref[...], preferred_element_type=F32, precision=lax.Precision.HIGHEST)
    logits = logits + br_ref[...]
    bm = logits.shape[0]
    lane = lax.broadcasted_iota(jnp.int32, (bm, LANES), 1)
    lanef = lane.astype(F32)
    neg = jnp.float32(-jnp.inf)
    big = jnp.float32(LANES)

    def top(mask):
        v = jnp.max(jnp.where(mask, logits, neg), axis=-1, keepdims=True)
        idx = jnp.min(jnp.where(jnp.logical_and(mask, logits == v), lanef, big), axis=-1, keepdims=True)
        return v, idx

    gmask = lane < N_GROUPS
    gmax, gidx = top(gmask)
    gsum = jnp.sum(jnp.where(gmask, jnp.exp(logits - gmax), 0.0), axis=-1, keepdims=True)
    g_top_p = 1.0 / gsum
    lo = N_GROUPS + gidx * EXPERTS_PER_GROUP
    emask = jnp.logical_and(lanef >= lo, lanef < lo + EXPERTS_PER_GROUP)
    v1, i1 = top(emask)
    v2, i2 = top(jnp.logical_and(emask, lanef != i1))
    r = jnp.exp(v2 - v1)
    den = 1.0 + r
    gate1 = g_top_p * (1.0 / den)
    gate2 = g_top_p * (r / den)
    meta = jnp.where(lane == 0, i1 - N_GROUPS,
                     jnp.where(lane == 1, i2 - N_GROUPS,
                               jnp.where(lane == 2, gate1, jnp.where(lane == 3, gate2, 0.0))))
    meta_ref[...] = meta


def _router(h, g, w_r, b_r):
    t, d = h.shape
    bm = _pick(t, 256)
    vmem = 2 * (bm * d * 4 + d * LANES * 4) + 4 * bm * d * 4
    return pl.pallas_call(
        _router_kernel,
        out_shape=jax.ShapeDtypeStruct((t, LANES), F32),
        grid=(t // bm,),
        in_specs=[pl.BlockSpec((bm, d), lambda i: (i, 0)),
                  pl.BlockSpec((1, d), lambda i: (0, 0)),
                  pl.BlockSpec((d, LANES), lambda i: (0, 0)),
                  pl.BlockSpec((1, LANES), lambda i: (0, 0))],
        out_specs=pl.BlockSpec((bm, LANES), lambda i: (i, 0)),
        compiler_params=_params(("parallel",), vmem),
        name="moe_router",
    )(h, g, w_r, b_r)


def _f_index(blk, f, nused):
    last = MOE_F_SPLIT - 1
    alt = jnp.where(blk % 2 == 0, f, last - f)
    parked = jnp.where((nused - 1) % 2 == 0, last, 0)
    return jnp.where(blk < nused, alt, parked)


def _expert_kernel(be_ref, tok_ref, nused_ref, h_hbm, g_ref, wg_ref, wu_ref, wd_ref, out_ref,
                   xraw, xb, sem):
    blk = pl.program_id(0)
    f = pl.program_id(1)
    nused = nused_ref[0]
    rows = xb.shape[0]

    def gather(b, slot):
        def body(r, carry):
            t = tok_ref[b * rows + r]
            pltpu.make_async_copy(h_hbm.at[pl.ds(t, 1), :], xraw.at[slot, pl.ds(r, 1), :],
                                  sem.at[slot]).start()
            return carry
        lax.fori_loop(0, rows, body, 0)

    @pl.when(blk < nused)
    def _():
        @pl.when(f == 0)
        def _():
            slot = blk % 2

            @pl.when(blk == 0)
            def _():
                gather(0, 0)

            @pl.when(blk + 1 < nused)
            def _():
                gather(blk + 1, 1 - slot)

            pltpu.make_async_copy(h_hbm.at[pl.ds(0, rows), :], xraw.at[slot], sem.at[slot]).wait()
            x = xraw[slot]
            xn = x * lax.rsqrt(jnp.mean(x * x, axis=-1, keepdims=True) + EPS) * g_ref[...]
            xb[...] = xn.astype(BF16)

        x = xb[...]
        hg = jnp.dot(x, wg_ref[0].astype(BF16), preferred_element_type=F32)
        hu = jnp.dot(x, wu_ref[0].astype(BF16), preferred_element_type=F32)
        hb = hg * (1.0 / (1.0 + jnp.exp(-hg))) * hu
        part = jnp.dot(hb.astype(BF16), wd_ref[0].astype(BF16), preferred_element_type=F32)

        @pl.when(f == 0)
        def _():
            out_ref[...] = part

        @pl.when(f > 0)
        def _():
            out_ref[...] += part


def _experts(h_all, g_ffn, w_gate, w_up, w_down, block_expert, buf_tok, nused, n_blocks):
    t, d = h_all.shape
    de = w_gate.shape[2]
    fs = de // MOE_F_SPLIT
    rows = MOE_ROWS

    def w_in_map(blk, f, be, tok, nu):
        return (be[blk], 0, _f_index(blk, f, nu[0]))

    def w_down_map(blk, f, be, tok, nu):
        return (be[blk], _f_index(blk, f, nu[0]), 0)

    gs = pltpu.PrefetchScalarGridSpec(
        num_scalar_prefetch=3,
        grid=(n_blocks, MOE_F_SPLIT),
        in_specs=[pl.BlockSpec(memory_space=pl.ANY),
                  pl.BlockSpec((1, d), lambda blk, f, be, tok, nu: (0, 0)),
                  pl.BlockSpec((1, d, fs), w_in_map),
                  pl.BlockSpec((1, d, fs), w_in_map),
                  pl.BlockSpec((1, fs, d), w_down_map)],
        out_specs=pl.BlockSpec((rows, d), lambda blk, f, be, tok, nu: (jnp.minimum(blk, nu[0] - 1), 0)),
        scratch_shapes=[pltpu.VMEM((2, rows, d), F32), pltpu.VMEM((rows, d), BF16),
                        pltpu.SemaphoreType.DMA((2,))],
    )
    vmem = 2 * 3 * d * fs * 4 + 2 * rows * d * 4 + 2 * rows * d * 4 + rows * d * 2 + 3 * d * fs * 2 \
        + 2 * rows * d * 4
    return pl.pallas_call(
        _expert_kernel,
        out_shape=jax.ShapeDtypeStruct((n_blocks * rows, d), F32),
        grid_spec=gs,
        compiler_params=_params(("arbitrary", "arbitrary"), vmem),
        name="moe_experts",
    )(block_expert, buf_tok, nused, h_all, g_ffn, w_gate, w_up, w_down)


def _combine_kernel(slot_ref, h_ref, meta_ref, gfin_ref, eo_hbm, y_ref, obuf, sem, *, tile0):
    i = pl.program_id(0)
    n = pl.num_programs(0)
    rows = h_ref.shape[0]

    def gather(tile, buf):
        base = (tile0 + tile) * rows * 2

        def body(r, carry):
            for k in range(2):
                s = slot_ref[base + 2 * r + k]
                pltpu.make_async_copy(eo_hbm.at[pl.ds(s, 1), :], obuf.at[buf, k, pl.ds(r, 1), :],
                                      sem.at[buf]).start()
            return carry
        lax.fori_loop(0, rows, body, 0)

    buf = i % 2

    @pl.when(i == 0)
    def _():
        gather(0, 0)

    @pl.when(i + 1 < n)
    def _():
        gather(i + 1, 1 - buf)

    for k in range(2):
        pltpu.make_async_copy(eo_hbm.at[pl.ds(0, rows), :], obuf.at[buf, k], sem.at[buf]).wait()
    meta = meta_ref[...]
    y = meta[:, 2:3] * obuf[buf, 0] + meta[:, 3:4] * obuf[buf, 1]
    hh = h_ref[...] + y
    hn = hh * lax.rsqrt(jnp.mean(hh * hh, axis=-1, keepdims=True) + EPS)
    y_ref[...] = hn * gfin_ref[...]


def _combine(h_all, meta, g_final, expert_out, slots, tok0, n_tok):
    d = h_all.shape[1]
    rows = _pick(n_tok, COMBINE_ROWS)
    assert tok0 % rows == 0 and n_tok % rows == 0
    tile0 = tok0 // rows
    gs = pltpu.PrefetchScalarGridSpec(
        num_scalar_prefetch=1,
        grid=(n_tok // rows,),
        in_specs=[pl.BlockSpec((rows, d), lambda i, sl: (i + tile0, 0)),
                  pl.BlockSpec((rows, LANES), lambda i, sl: (i + tile0, 0)),
                  pl.BlockSpec((1, d), lambda i, sl: (0, 0)),
                  pl.BlockSpec(memory_space=pl.ANY)],
        out_specs=pl.BlockSpec((rows, d), lambda i, sl: (i, 0)),
        scratch_shapes=[pltpu.VMEM((2, 2, rows, d), F32), pltpu.SemaphoreType.DMA((2,))],
    )
    vmem = 4 * rows * d * 4 + 4 * rows * d * 4 + 4 * rows * d * 4
    return pl.pallas_call(
        functools.partial(_combine_kernel, tile0=tile0),
        out_shape=jax.ShapeDtypeStruct((n_tok, d), F32),
        grid_spec=gs,
        compiler_params=_params(("arbitrary",), vmem),
        name="moe_combine",
    )(slots, h_all, meta, g_final, expert_out)


def _dispatch_plan(meta, n_blocks):
    t = meta.shape[0]
    e_flat = meta[:, 0:2].astype(jnp.int32).reshape(-1)
    onehot = (e_flat[:, None] == jnp.arange(N_EXPERTS, dtype=jnp.int32)[None, :]).astype(jnp.int32)
    csum = jnp.cumsum(onehot, axis=0)
    rank = jnp.take_along_axis(csum, e_flat[:, None], axis=1)[:, 0] - 1
    counts = csum[-1]
    nblk = (counts + MOE_ROWS - 1) // MOE_ROWS
    blk_end = jnp.cumsum(nblk)
    slots = (blk_end - nblk)[e_flat] * MOE_ROWS + rank
    nused = blk_end[-1]
    tok = jnp.arange(2 * t, dtype=jnp.int32) // 2
    buf_tok = jnp.zeros((n_blocks * MOE_ROWS,), jnp.int32).at[slots].set(tok)
    blk_ids = jnp.minimum(jnp.arange(n_blocks, dtype=jnp.int32), nused - 1)
    block_expert = jnp.minimum(jnp.searchsorted(blk_end, blk_ids, side="right"), N_EXPERTS - 1)
    return (block_expert.astype(jnp.int32), buf_tok, nused.reshape(1).astype(jnp.int32),
            slots.astype(jnp.int32))


def kernel(x_prompt, x_sample, cache_k, cache_v, state_conv, cache_mem_k, cache_mem_v, page_table, mem_prompt, g_mix, w_in, lambda_q1, lambda_k1, lambda_q2, lambda_k2, g_subln, w_conv, w_out, g_mem, w_ck, w_cv, g_cross, w_cq, w_co, g_ffn, w_router_group, b_router_group, w_router_expert, b_router_expert, w_e_gate, w_e_up, w_e_down, g_final):
    depth = g_mix.shape[0]
    assert depth == 1, "single-layer trunk only"
    batch, seq, d = x_prompt.shape
    bd, dec_seq, _ = x_sample.shape
    assert dec_seq == 1
    n_pool, page = cache_k.shape[1], cache_k.shape[2]
    n_heads = cache_k.shape[3]
    aw = n_heads * HEAD_DIM
    cw = w_conv.shape[2]
    cross_w = w_cq.shape[2]
    n_mem = mem_prompt.shape[1]
    past_len = page_table.shape[1] * page
    lam_init = 0.8 - 0.6 * math.exp(-0.3 * 0)
    tp, ts = batch * seq, bd

    w_in_b = w_in.reshape(w_in.shape[1:]).astype(BF16)
    w_out_b = w_out.reshape(w_out.shape[1:]).astype(BF16)
    w_ck_b, w_cv_b, w_cq_b, w_co_b = (w.reshape(w.shape[1:]).astype(BF16) for w in (w_ck, w_cv, w_cq, w_co))
    lams = tuple(v.reshape(1, QK_DIM) for v in (lambda_q1[0], lambda_k1[0], lambda_q2[0], lambda_k2[0]))
    gsub = g_subln[0].reshape(1, HEAD_DIM)
    attn_scale = QK_DIM ** -0.5
    cross_scale = (cross_w // N_CROSS_HEADS) ** -0.5
    col_q, col_k, col_v = 0, aw, 2 * aw
    col_b, col_c, col_x = 3 * aw, 3 * aw + cw, 3 * aw + 2 * cw

    def in_proj(a, rope, period):
        q = _matmul([a], w_in_b, col_q, aw, out_dtype=BF16, mode="rope", scale=attn_scale, rope=rope,
                    rope_period=period, name="in_proj_q")
        k = _matmul([a], w_in_b, col_k, aw, out_dtype=F32, mode="rope", rope=rope, rope_period=period,
                    name="in_proj_k")
        v = _matmul([a], w_in_b, col_v, aw, out_dtype=F32, name="in_proj_v")
        return q, k, v

    def after_mixer(x2d, o_attn, g_conv, mem_k, mem_v, cross_fn):
        h1 = _matmul([o_attn, g_conv], w_out_b, 0, d, out_dtype=F32, mode="residual", residual=x2d,
                     bn_pref=512, name="out_proj")
        a2 = _rmsnorm_bf16(h1, g_cross[0])
        qc = _matmul([a2], w_cq_b, 0, cross_w, out_dtype=BF16, scale=cross_scale, name="cross_q")
        oc = cross_fn(qc, mem_k, mem_v)
        return _matmul([oc], w_co_b, 0, d, out_dtype=F32, mode="residual", residual=h1, bn_pref=512,
                       name="cross_out")

    xp = x_prompt.reshape(tp, d)
    a_p = _rmsnorm_bf16(xp, g_mix[0])
    rope_p = _rope_tables(jnp.arange(seq, dtype=jnp.int32))
    q_p, k_p, v_p = in_proj(a_p, rope_p, seq)
    o_p = _causal_diff_attn(q_p.reshape(batch, seq, aw), k_p.reshape(batch, seq, aw),
                            v_p.reshape(batch, seq, aw), lams, gsub, lam_init).reshape(tp, aw)
    g_p, conv_p = _conv_branch_seq(a_p, w_in_b, col_b, col_c, col_x, w_conv[0], batch, seq)
    m_p = _rmsnorm_bf16(mem_prompt.reshape(batch * n_mem, d), g_mem[0])
    mem_k_p = _matmul([m_p], w_ck_b, 0, cross_w, out_dtype=F32, name="mem_k")
    mem_v_p = _matmul([m_p], w_cv_b, 0, cross_w, out_dtype=F32, name="mem_v")
    h2_p = after_mixer(
        xp, o_p, g_p, mem_k_p.reshape(batch, n_mem, cross_w), mem_v_p.reshape(batch, n_mem, cross_w),
        lambda qc, mk, mv: _cross_attn_seq(qc.reshape(batch, seq, cross_w), mk, mv).reshape(tp, cross_w))

    xs = x_sample.reshape(ts, d)
    a_s = _rmsnorm_bf16(xs, g_mix[0])
    rope_s = _rope_tables(jnp.full((ts,), past_len, dtype=jnp.int32))
    q_s, k_s, v_s = in_proj(a_s, rope_s, ts)
    o_s = _paged_diff_attn(q_s, k_s, v_s, cache_k.reshape(n_pool, page, aw), cache_v.reshape(n_pool, page, aw),
                           page_table, lams, gsub, lam_init)
    st = state_conv[0]
    g_s, z_s = _conv_branch_step(a_s, w_in_b, col_b, col_c, col_x, w_conv[0], st[:, 0, :], st[:, 1, :])
    conv_s = jnp.stack([st[:, 1, :], z_s], axis=1)
    h2_s = after_mixer(xs, o_s, g_s, cache_mem_k.reshape(bd, n_mem, cross_w),
                       cache_mem_v.reshape(bd, n_mem, cross_w), _cross_attn_step)

    h_all = jnp.concatenate([h2_p, h2_s], axis=0)
    t_all = tp + ts
    w_r = jnp.concatenate([w_router_group[0], w_router_expert[0],
                           jnp.zeros((d, LANES - N_GROUPS - N_EXPERTS), F32)], axis=1)
    b_r = jnp.concatenate([b_router_group[0], b_router_expert[0],
                           jnp.zeros((LANES - N_GROUPS - N_EXPERTS,), F32)]).reshape(1, LANES)
    g_ffn2 = g_ffn[0].reshape(1, d)
    meta = _router(h_all, g_ffn2, w_r, b_r)
    n_blocks = (2 * t_all + N_EXPERTS * (MOE_ROWS - 1) + MOE_ROWS - 1) // MOE_ROWS
    block_expert, buf_tok, nused, slots = _dispatch_plan(meta, n_blocks)
    expert_out = _experts(h_all, g_ffn2, w_e_gate.reshape(w_e_gate.shape[1:]), w_e_up.reshape(w_e_up.shape[1:]),
                          w_e_down.reshape(w_e_down.shape[1:]), block_expert, buf_tok, nused,
                          n_blocks)
    g_fin = g_final.reshape(1, d)
    y_p = _combine(h_all, meta, g_fin, expert_out, slots, 0, tp)
    y_s = _combine(h_all, meta, g_fin, expert_out, slots, tp, ts)

    return (y_p.reshape(batch, seq, d), y_s.reshape(bd, 1, d),
            k_p.reshape(1, batch, seq, n_heads, HEAD_DIM), v_p.reshape(1, batch, seq, n_heads, HEAD_DIM),
            conv_p.reshape(1, batch, CONV_K - 1, cw),
            mem_k_p.reshape(1, batch, n_mem, N_CROSS_HEADS, cross_w // N_CROSS_HEADS),
            mem_v_p.reshape(1, batch, n_mem, N_CROSS_HEADS, cross_w // N_CROSS_HEADS),
            k_s.reshape(1, bd, 1, n_heads, HEAD_DIM), v_s.reshape(1, bd, 1, n_heads, HEAD_DIM),
            conv_s.reshape(1, bd, CONV_K - 1, cw))
```

```python
import functools
import math

import jax
import jax.numpy as jnp
from jax import lax
from jax.experimental import pallas as pl
from jax.experimental.pallas import tpu as pltpu

F32 = jnp.float32
BF16 = jnp.bfloat16

EPS = 1e-6
HEAD_DIM = 128
QK_DIM = HEAD_DIM // 2
ROPE_DIM = QK_DIM // 4
ROPE_HALF = ROPE_DIM // 2
ROPE_THETA = 500000.0
N_CROSS_HEADS = 4
N_GROUPS = 8
EXPERTS_PER_GROUP = 8
N_EXPERTS = N_GROUPS * EXPERTS_PER_GROUP
CONV_K = 3

LANES = 128
V7X_VMEM_BYTES = 64 * 1024 * 1024
VMEM_CAP = V7X_VMEM_BYTES - 6 * 1024 * 1024
MOE_ROWS = 512
MOE_SUB = 128
MOE_CHUNKS = 4
COMBINE_ROWS = 128


def _pick(n, pref):
    if n <= pref:
        return n
    for d in range(pref, 7, -1):
        if n % d == 0 and d % 8 == 0:
            return d
    return n


def _params(sem, vmem_est):
    limit = int(min(VMEM_CAP, max(32 * 1024 * 1024, vmem_est * 5 // 4 + (2 << 20))))
    return pltpu.CompilerParams(dimension_semantics=sem, vmem_limit_bytes=limit)


def _nt_dot(a, b):
    return lax.dot_general(a, b, (((1,), (1,)), ((), ())), preferred_element_type=F32)


def _rmsnorm_kernel(x_ref, g_ref, o_ref):
    x = x_ref[...]
    y = x * lax.rsqrt(jnp.mean(x * x, axis=-1, keepdims=True) + EPS)
    o_ref[...] = (y * g_ref[...]).astype(o_ref.dtype)


def _rmsnorm_bf16(x, g):
    m, d = x.shape
    bm = _pick(m, 256)
    return pl.pallas_call(
        _rmsnorm_kernel,
        out_shape=jax.ShapeDtypeStruct((m, d), BF16),
        grid=(m // bm,),
        in_specs=[pl.BlockSpec((bm, d), lambda i: (i, 0)),
                  pl.BlockSpec((1, d), lambda i: (0, 0))],
        out_specs=pl.BlockSpec((bm, d), lambda i: (i, 0)),
        compiler_params=_params(("parallel",), 2 * bm * d * 6),
        name="rmsnorm_bf16",
    )(x, g.reshape(1, d))


def _mm_kernel(*refs, k_splits, mode, scale):
    n_lhs = len(k_splits)
    lhs = refs[:n_lhs]
    w_ref = refs[n_lhs]
    extra = refs[n_lhs + 1:-1]
    o_ref = refs[-1]
    acc = None
    k0 = 0
    for a_ref, kk in zip(lhs, k_splits):
        part = jnp.dot(a_ref[...], w_ref[k0:k0 + kk, :], preferred_element_type=F32)
        acc = part if acc is None else acc + part
        k0 += kk
    if mode == "rope":
        cos_ref, sa_ref, sb_ref = extra
        cos, sa, sb = cos_ref[...], sa_ref[...], sb_ref[...]
        for c in range(acc.shape[1] // LANES):
            blk = acc[:, c * LANES:(c + 1) * LANES]
            up = pltpu.roll(blk, LANES - ROPE_HALF, axis=1)
            dn = pltpu.roll(blk, ROPE_HALF, axis=1)
            rot = blk * cos + up * sa + dn * sb
            if scale != 1.0:
                rot = rot * scale
            o_ref[:, c * LANES:(c + 1) * LANES] = rot.astype(o_ref.dtype)
        return
    if mode == "residual":
        acc = extra[0][...] + acc
    if scale != 1.0:
        acc = acc * scale
    o_ref[...] = acc.astype(o_ref.dtype)


def _matmul(lhs_list, w, col0, n, *, out_dtype, mode="plain", scale=1.0, rope=None,
            rope_period=None, residual=None, bm_pref=1024, bn_pref=1024, name="matmul"):
    m = lhs_list[0].shape[0]
    k_splits = tuple(a.shape[1] for a in lhs_list)
    k_total = sum(k_splits)
    assert w.shape[0] == k_total
    bm = _pick(m, bm_pref)
    bn = _pick(n, bn_pref)
    assert col0 % bn == 0 and n % bn == 0 and m % bm == 0
    joff = col0 // bn
    in_specs = [pl.BlockSpec((bm, kk), lambda i, j: (i, 0)) for kk in k_splits]
    in_specs.append(pl.BlockSpec((k_total, bn), lambda i, j: (0, j + joff)))
    args = list(lhs_list) + [w]
    vmem = 2 * (bm * k_total * 2 + k_total * bn * 2 + bm * bn * 4) + 2 * bm * bn * 4
    if mode == "rope":
        assert rope_period % bm == 0 or bm % rope_period == 0
        if bm > rope_period:
            rope = tuple(jnp.tile(t, (bm // rope_period, 1)) for t in rope)
            nper = 1
        else:
            nper = rope_period // bm
        in_specs += [pl.BlockSpec((bm, LANES), lambda i, j: (i % nper, 0))] * 3
        args += list(rope)
    elif mode == "residual":
        in_specs.append(pl.BlockSpec((bm, bn), lambda i, j: (i, j)))
        args.append(residual)
        vmem += 2 * bm * bn * 4
    return pl.pallas_call(
        functools.partial(_mm_kernel, k_splits=k_splits, mode=mode, scale=scale),
        out_shape=jax.ShapeDtypeStruct((m, n), out_dtype),
        grid=(m // bm, n // bn),
        in_specs=in_specs,
        out_specs=pl.BlockSpec((bm, bn), lambda i, j: (i, j)),
        compiler_params=_params(("parallel", "parallel"), vmem),
        name=name,
    )(*args)


def _rope_tables(pos):
    inv_freq = ROPE_THETA ** (-jnp.arange(ROPE_HALF, dtype=F32) * (2.0 / ROPE_DIM))
    ang = pos.astype(F32)[:, None] * inv_freq[None, :]
    cos, sin = jnp.cos(ang), jnp.sin(ang)
    n = pos.shape[0]
    one = jnp.ones((n, QK_DIM - ROPE_DIM), F32)
    zero = jnp.zeros((n, QK_DIM - ROPE_DIM), F32)
    z8 = jnp.zeros((n, ROPE_HALF), F32)
    cos64 = jnp.concatenate([cos, cos, one], axis=1)
    up64 = jnp.concatenate([-sin, z8, zero], axis=1)
    dn64 = jnp.concatenate([z8, sin, zero], axis=1)
    rep = LANES // QK_DIM
    return tuple(jnp.tile(t, (1, rep)) for t in (cos64, up64, dn64))


def _conv_seq_kernel(a_ref, wb_ref, wc_ref, wx_ref, wconv_ref, g_ref, nc_ref, zbuf, *, tiles_per_seq):
    i = pl.program_id(1)
    bm = a_ref.shape[0]
    a = a_ref[...]
    cb = jnp.dot(a, wb_ref[...], preferred_element_type=F32)
    cc = jnp.dot(a, wc_ref[...], preferred_element_type=F32)
    cx = jnp.dot(a, wx_ref[...], preferred_element_type=F32)
    z = cc * cx
    first = (i % tiles_per_seq) == 0

    @pl.when(first)
    def _():
        zbuf[0:8, :] = jnp.zeros((8, zbuf.shape[1]), F32)

    @pl.when(jnp.logical_not(first))
    def _():
        zbuf[0:8, :] = zbuf[bm:bm + 8, :]

    zbuf[8:8 + bm, :] = z
    w = wconv_ref[...]
    y = w[0:1, :] * zbuf[6:6 + bm, :] + w[1:2, :] * zbuf[7:7 + bm, :]
    y = y + w[2:3, :] * z
    g_ref[...] = (cb * y).astype(g_ref.dtype)

    @pl.when((i % tiles_per_seq) == tiles_per_seq - 1)
    def _():
        nc_ref[0] = z[bm - (CONV_K - 1):, :]


def _conv_branch_seq(a, w, col_b, col_c, col_x, w_conv, batch, seq):
    m, k = a.shape
    cw = w_conv.shape[1]
    assert seq >= 8
    bm = _pick(seq, 512)
    bc = _pick(cw, 512)
    tps = seq // bm
    ob, oc, ox = col_b // bc, col_c // bc, col_x // bc
    vmem = 2 * (bm * k * 2 + 3 * k * bc * 2 + bm * bc * 2) + 6 * bm * bc * 4
    return pl.pallas_call(
        functools.partial(_conv_seq_kernel, tiles_per_seq=tps),
        out_shape=(jax.ShapeDtypeStruct((m, cw), BF16),
                   jax.ShapeDtypeStruct((batch, CONV_K - 1, cw), F32)),
        grid=(cw // bc, m // bm),
        in_specs=[pl.BlockSpec((bm, k), lambda c, i: (i, 0)),
                  pl.BlockSpec((k, bc), lambda c, i: (0, ob + c)),
                  pl.BlockSpec((k, bc), lambda c, i: (0, oc + c)),
                  pl.BlockSpec((k, bc), lambda c, i: (0, ox + c)),
                  pl.BlockSpec((CONV_K, bc), lambda c, i: (0, c))],
        out_specs=(pl.BlockSpec((bm, bc), lambda c, i: (i, c)),
                   pl.BlockSpec((1, CONV_K - 1, bc), lambda c, i: (i // tps, 0, c))),
        scratch_shapes=[pltpu.VMEM((bm + 8, bc), F32)],
        compiler_params=_params(("parallel", "arbitrary"), vmem),
        name="conv_branch_seq",
    )(a, w, w, w, w_conv)


def _conv_step_kernel(a_ref, wb_ref, wc_ref, wx_ref, wconv_ref, s0_ref, s1_ref, g_ref, z_ref):
    a = a_ref[...]
    cb = jnp.dot(a, wb_ref[...], preferred_element_type=F32)
    cc = jnp.dot(a, wc_ref[...], preferred_element_type=F32)
    cx = jnp.dot(a, wx_ref[...], preferred_element_type=F32)
    z = cc * cx
    w = wconv_ref[...]
    y = w[0:1, :] * s0_ref[...] + w[1:2, :] * s1_ref[...]
    y = y + w[2:3, :] * z
    g_ref[...] = (cb * y).astype(g_ref.dtype)
    z_ref[...] = z


def _conv_branch_step(a, w, col_b, col_c, col_x, w_conv, s0, s1):
    m, k = a.shape
    cw = w_conv.shape[1]
    bc = _pick(cw, 512)
    ob, oc, ox = col_b // bc, col_c // bc, col_x // bc
    vmem = 2 * (m * k * 2 + 3 * k * bc * 2 + 5 * m * bc * 4)
    return pl.pallas_call(
        _conv_step_kernel,
        out_shape=(jax.ShapeDtypeStruct((m, cw), BF16), jax.ShapeDtypeStruct((m, cw), F32)),
        grid=(cw // bc,),
        in_specs=[pl.BlockSpec((m, k), lambda c: (0, 0)),
                  pl.BlockSpec((k, bc), lambda c: (0, ob + c)),
                  pl.BlockSpec((k, bc), lambda c: (0, oc + c)),
                  pl.BlockSpec((k, bc), lambda c: (0, ox + c)),
                  pl.BlockSpec((CONV_K, bc), lambda c: (0, c)),
                  pl.BlockSpec((m, bc), lambda c: (0, c)),
                  pl.BlockSpec((m, bc), lambda c: (0, c))],
        out_specs=(pl.BlockSpec((m, bc), lambda c: (0, c)),
                   pl.BlockSpec((m, bc), lambda c: (0, c))),
        compiler_params=_params(("parallel",), vmem),
        name="conv_branch_step",
    )(a, w, w, w, w_conv, s0, s1)


def _diff_lambda(lq1_ref, lk1_ref, lq2_ref, lk2_ref, lam_init):
    a = jnp.sum(lq1_ref[...] * lk1_ref[...], axis=-1, keepdims=True)
    b = jnp.sum(lq2_ref[...] * lk2_ref[...], axis=-1, keepdims=True)
    return jnp.exp(a) - jnp.exp(b) + lam_init


def _subln(d, gsub, lam_init):
    dn = d * lax.rsqrt(jnp.mean(d * d, axis=-1, keepdims=True) + EPS)
    return dn * gsub * (1.0 - lam_init)


def _causal_attn_kernel(q_ref, k_ref, v_ref, lq1, lk1, lq2, lk2, gsub_ref, o_ref, kb, vb, *, bq, lam_init):
    seq = q_ref.shape[1]
    kb[...] = k_ref[0].astype(BF16)
    vb[...] = v_ref[0].astype(BF16)
    lam = _diff_lambda(lq1, lk1, lq2, lk2, lam_init)
    gsub = gsub_ref[...]
    lane = lax.broadcasted_iota(jnp.int32, (bq, HEAD_DIM), 1)
    row = lax.broadcasted_iota(jnp.int32, (2 * bq, bq), 0)
    col = lax.broadcasted_iota(jnp.int32, (2 * bq, bq), 1)
    tri = col <= jnp.where(row >= bq, row - bq, row)
    for qi in range(seq // bq):
        q = q_ref[0, qi * bq:(qi + 1) * bq, :]
        zero = jnp.zeros_like(q)
        qs = jnp.concatenate([jnp.where(lane < QK_DIM, q, zero), jnp.where(lane >= QK_DIM, q, zero)], axis=0)
        past = qi * bq
        s_diag = jnp.where(tri, _nt_dot(qs, kb[past:past + bq, :]), -jnp.inf)
        m = jnp.max(s_diag, axis=-1, keepdims=True)
        if past:
            s_past = _nt_dot(qs, kb[0:past, :])
            m = jnp.maximum(m, jnp.max(s_past, axis=-1, keepdims=True))
        p_diag = jnp.exp(s_diag - m)
        l = jnp.sum(p_diag, axis=-1, keepdims=True)
        o = jnp.dot(p_diag.astype(BF16), vb[past:past + bq, :], preferred_element_type=F32)
        if past:
            p_past = jnp.exp(s_past - m)
            l = l + jnp.sum(p_past, axis=-1, keepdims=True)
            o = o + jnp.dot(p_past.astype(BF16), vb[0:past, :], preferred_element_type=F32)
        o = o / l
        d = o[:bq] - lam * o[bq:]
        o_ref[0, qi * bq:(qi + 1) * bq, :] = _subln(d, gsub, lam_init).astype(o_ref.dtype)


def _causal_diff_attn(q, k, v, lams, gsub, lam_init):
    b, s, w = q.shape
    h = w // HEAD_DIM
    bq = _pick(s, 256)
    blk = lambda: pl.BlockSpec((1, s, HEAD_DIM), lambda bi, hi: (bi, 0, hi))
    small = lambda n: pl.BlockSpec((1, n), lambda bi, hi: (0, 0))
    vmem = 2 * s * HEAD_DIM * (2 + 4 + 4 + 2) + 2 * s * HEAD_DIM * 2 + 6 * 2 * bq * s * 4
    return pl.pallas_call(
        functools.partial(_causal_attn_kernel, bq=bq, lam_init=lam_init),
        out_shape=jax.ShapeDtypeStruct((b, s, w), BF16),
        grid=(b, h),
        in_specs=[blk(), blk(), blk(), small(QK_DIM), small(QK_DIM), small(QK_DIM), small(QK_DIM),
                  small(HEAD_DIM)],
        out_specs=blk(),
        scratch_shapes=[pltpu.VMEM((s, HEAD_DIM), BF16), pltpu.VMEM((s, HEAD_DIM), BF16)],
        compiler_params=_params(("parallel", "parallel"), vmem),
        name="causal_diff_attn",
    )(q, k, v, *lams, gsub)


def _paged_attn_kernel(pt_ref, q_ref, kc_ref, vc_ref, bias_ref, *rest, n_heads, ppb, lam_init):
    kp_refs, vp_refs = rest[:ppb], rest[ppb:2 * ppb]
    lq1, lk1, lq2, lk2, gsub_ref, o_ref, qm, m_s, l_s, acc = rest[2 * ppb:]
    p = pl.program_id(1)
    page = kp_refs[0].shape[1]

    @pl.when(p == 0)
    def _():
        q = q_ref[0].astype(F32)
        lane = lax.broadcasted_iota(jnp.int32, q.shape, 1)
        qm[0:n_heads, :] = jnp.where(lane < QK_DIM, q, 0.0).astype(BF16)
        qm[n_heads:, :] = jnp.where(lane >= QK_DIM, q, 0.0).astype(BF16)
        m_s[...] = jnp.full(m_s.shape, -jnp.inf, F32)
        l_s[...] = jnp.zeros(l_s.shape, F32)
        acc[...] = jnp.zeros(acc.shape, F32)

    qv = qm[...]
    bias = bias_ref[...]
    ss = [_nt_dot(qv, kp_refs[j][0].reshape(page * n_heads, HEAD_DIM).astype(BF16)) + bias
          for j in range(ppb)]
    m_old = m_s[...]
    m_new = m_old
    for s in ss:
        m_new = jnp.maximum(m_new, jnp.max(s, axis=-1, keepdims=True))
    corr = jnp.exp(m_old - m_new)
    l = l_s[...] * corr
    a = acc[...] * corr
    for j, s in enumerate(ss):
        pr = jnp.exp(s - m_new)
        l = l + jnp.sum(pr, axis=-1, keepdims=True)
        a = a + jnp.dot(pr.astype(BF16), vp_refs[j][0].reshape(page * n_heads, HEAD_DIM).astype(BF16),
                        preferred_element_type=F32)
    l_s[...] = l
    acc[...] = a
    m_s[...] = m_new

    @pl.when(p == pl.num_programs(1) - 1)
    def _():
        kc = kc_ref[0].astype(BF16).astype(F32)
        vc = vc_ref[0].astype(BF16).astype(F32)
        kc2 = jnp.concatenate([kc, kc], axis=0)
        vc2 = jnp.concatenate([vc, vc], axis=0)
        s_c = jnp.sum(qm[...].astype(F32) * kc2, axis=-1, keepdims=True)
        m_old = m_s[...]
        m_new = jnp.maximum(m_old, s_c)
        corr = jnp.exp(m_old - m_new)
        p_c = jnp.exp(s_c - m_new)
        l = l_s[...] * corr + p_c
        a = acc[...] * corr + p_c.astype(BF16).astype(F32) * vc2
        o = a / l
        lam = _diff_lambda(lq1, lk1, lq2, lk2, lam_init)
        d = o[:n_heads] - lam * o[n_heads:]
        o_ref[0] = _subln(d, gsub_ref[...], lam_init).astype(o_ref.dtype)


def _paged_diff_attn(q, k_cur, v_cur, cache_k, cache_v, page_table, lams, gsub, lam_init):
    bd, n_heads, _ = q.shape
    n_pages = page_table.shape[1]
    page = cache_k.shape[1]
    ppb = 4 if n_pages % 4 == 0 else (2 if n_pages % 2 == 0 else 1)
    nrow = 2 * n_heads
    ncol = page * n_heads
    own = ((jnp.arange(ncol, dtype=jnp.int32)[None, :] % n_heads)
           == (jnp.arange(nrow, dtype=jnp.int32)[:, None] % n_heads))
    bias = jnp.where(own, 0.0, -jnp.inf).astype(F32)
    row = lambda: pl.BlockSpec((1, n_heads, HEAD_DIM), lambda b, p, pt: (b, 0, 0))

    def pg(j):
        return pl.BlockSpec((1, page, n_heads, HEAD_DIM),
                            lambda b, p, pt: (pt[b * n_pages + p * ppb + j], 0, 0, 0))

    small = lambda n: pl.BlockSpec((1, n), lambda b, p, pt: (0, 0))
    gs = pltpu.PrefetchScalarGridSpec(
        num_scalar_prefetch=1,
        grid=(bd, n_pages // ppb),
        in_specs=[row(), row(), row(), pl.BlockSpec((nrow, ncol), lambda b, p, pt: (0, 0))]
        + [pg(j) for j in range(ppb)] + [pg(j) for j in range(ppb)]
        + [small(QK_DIM), small(QK_DIM), small(QK_DIM), small(QK_DIM), small(HEAD_DIM)],
        out_specs=row(),
        scratch_shapes=[pltpu.VMEM((nrow, HEAD_DIM), BF16), pltpu.VMEM((nrow, 1), F32),
                        pltpu.VMEM((nrow, 1), F32), pltpu.VMEM((nrow, HEAD_DIM), F32)],
    )
    vmem = 4 * ppb * ncol * HEAD_DIM * 4 + 2 * ppb * ncol * HEAD_DIM * 2 + (4 + 3 * ppb) * nrow * ncol * 4
    return pl.pallas_call(
        functools.partial(_paged_attn_kernel, n_heads=n_heads, ppb=ppb, lam_init=lam_init),
        out_shape=jax.ShapeDtypeStruct((bd, n_heads, HEAD_DIM), BF16),
        grid_spec=gs,
        compiler_params=_params(("parallel", "arbitrary"), vmem),
        name="paged_diff_attn",
    )(page_table.reshape(-1), q, k_cur, v_cur, bias, *([cache_k] * ppb), *([cache_v] * ppb), *lams, gsub)


def _cross_attn_kernel(q_ref, mk_ref, mv_ref, o_ref, mkb, mvb):
    @pl.when(pl.program_id(1) == 0)
    def _():
        mkb[...] = mk_ref[0].astype(BF16)
        mvb[...] = mv_ref[0].astype(BF16)

    hd = q_ref.shape[2] // N_CROSS_HEADS
    for h in range(N_CROSS_HEADS):
        sl = slice(h * hd, (h + 1) * hd)
        s = _nt_dot(q_ref[0, :, sl], mkb[:, sl])
        p = jnp.exp(s - jnp.max(s, axis=-1, keepdims=True))
        l = jnp.sum(p, axis=-1, keepdims=True)
        o = jnp.dot(p.astype(BF16), mvb[:, sl], preferred_element_type=F32) / l
        o_ref[0, :, sl] = o.astype(o_ref.dtype)


def _cross_attn_seq(q, mem_k, mem_v):
    b, s, cw = q.shape
    n_mem = mem_k.shape[1]
    bq = _pick(s, 512)
    vmem = 2 * (2 * bq * cw * 2 + 2 * n_mem * cw * 4) + 2 * n_mem * cw * 2 + 6 * bq * n_mem * 4
    return pl.pallas_call(
        _cross_attn_kernel,
        out_shape=jax.ShapeDtypeStruct((b, s, cw), BF16),
        grid=(b, s // bq),
        in_specs=[pl.BlockSpec((1, bq, cw), lambda bi, i: (bi, i, 0)),
                  pl.BlockSpec((1, n_mem, cw), lambda bi, i: (bi, 0, 0)),
                  pl.BlockSpec((1, n_mem, cw), lambda bi, i: (bi, 0, 0))],
        out_specs=pl.BlockSpec((1, bq, cw), lambda bi, i: (bi, i, 0)),
        scratch_shapes=[pltpu.VMEM((n_mem, cw), BF16), pltpu.VMEM((n_mem, cw), BF16)],
        compiler_params=_params(("parallel", "arbitrary"), vmem),
        name="cross_attn_seq",
    )(q, mem_k, mem_v)


def _cross_step_kernel(q_ref, mk_hbm, mv_hbm, o_ref, kbuf, vbuf, sem):
    i = pl.program_id(0)
    n = pl.num_programs(0)
    bb, _, cw = q_ref.shape
    nh, hd = kbuf.shape[1], kbuf.shape[4]

    def copies(step, slot):
        out = []
        for h in range(nh):
            src = pl.ds(step * bb, bb)
            out.append(pltpu.make_async_copy(mk_hbm.at[src, :, h, :], kbuf.at[slot, h], sem.at[slot, 0]))
            out.append(pltpu.make_async_copy(mv_hbm.at[src, :, h, :], vbuf.at[slot, h], sem.at[slot, 1]))
        return out

    slot = i % 2

    @pl.when(i == 0)
    def _():
        for c in copies(0, 0):
            c.start()

    @pl.when(i + 1 < n)
    def _():
        for c in copies(i + 1, 1 - slot):
            c.start()

    for c in copies(i, slot):
        c.wait()
    for j in range(bb):
        for h in range(nh):
            sl = slice(h * hd, (h + 1) * hd)
            qh = jnp.broadcast_to(q_ref[j, :, sl], (8, hd))
            s = _nt_dot(qh, kbuf[slot, h, j].astype(BF16))
            p = jnp.exp(s - jnp.max(s, axis=-1, keepdims=True))
            l = jnp.sum(p, axis=-1, keepdims=True)
            o = jnp.dot(p.astype(BF16), vbuf[slot, h, j].astype(BF16), preferred_element_type=F32) / l
            o_ref[j, :, sl] = o[0:1].astype(o_ref.dtype)


def _cross_attn_step(q, mem_k, mem_v):
    bd, cw = q.shape
    n_mem, nh, hd = mem_k.shape[1:]
    assert nh == N_CROSS_HEADS and nh * hd == cw
    bb = 4 if bd % 4 == 0 else 1
    buf = pltpu.VMEM((2, nh, bb, n_mem, hd), F32)
    vmem = 2 * 2 * nh * bb * n_mem * hd * 4 + 4 * n_mem * hd * 4
    out = pl.pallas_call(
        _cross_step_kernel,
        out_shape=jax.ShapeDtypeStruct((bd, 1, cw), BF16),
        grid=(bd // bb,),
        in_specs=[pl.BlockSpec((bb, 1, cw), lambda i: (i, 0, 0)),
                  pl.BlockSpec(memory_space=pl.ANY), pl.BlockSpec(memory_space=pl.ANY)],
        out_specs=pl.BlockSpec((bb, 1, cw), lambda i: (i, 0, 0)),
        scratch_shapes=[buf, buf, pltpu.SemaphoreType.DMA((2, 2))],
        compiler_params=_params(("arbitrary",), vmem),
        name="cross_attn_step",
    )(q.reshape(bd, 1, cw), mem_k, mem_v)
    return out.reshape(bd, cw)


def _router_kernel(h_ref, g_ref, wr_ref, br_ref, meta_ref):
    x = h_ref[...]
    xn = x * lax.rsqrt(jnp.mean(x * x, axis=-1, keepdims=True) + EPS) * g_ref[...]
    logits = jnp.dot(xn, wr_ref[...], preferred_element_type=F32, precision=lax.Precision.HIGHEST)
    logits = logits + br_ref[...]
    bm = logits.shape[0]
    lane = lax.broadcasted_iota(jnp.int32, (bm, LANES), 1)
    lanef = lane.astype(F32)
    neg = jnp.float32(-jnp.inf)
    big = jnp.float32(LANES)

    def top(mask):
        v = jnp.max(jnp.where(mask, logits, neg), axis=-1, keepdims=True)
        idx = jnp.min(jnp.where(jnp.logical_and(mask, logits == v), lanef, big), axis=-1, keepdims=True)
        return v, idx

    gmask = lane < N_GROUPS
    gmax, gidx = top(gmask)
    gsum = jnp.sum(jnp.where(gmask, jnp.exp(logits - gmax), 0.0), axis=-1, keepdims=True)
    g_top_p = 1.0 / gsum
    lo = N_GROUPS + gidx * EXPERTS_PER_GROUP
    emask = jnp.logical_and(lanef >= lo, lanef < lo + EXPERTS_PER_GROUP)
    v1, i1 = top(emask)
    v2, i2 = top(jnp.logical_and(emask, lanef != i1))
    r = jnp.exp(v2 - v1)
    den = 1.0 + r
    gate1 = g_top_p * (1.0 / den)
    gate2 = g_top_p * (r / den)
    meta = jnp.where(lane == 0, i1 - N_GROUPS,
                     jnp.where(lane == 1, i2 - N_GROUPS,
                               jnp.where(lane == 2, gate1, jnp.where(lane == 3, gate2, 0.0))))
    meta_ref[...] = meta


def _router(h, g, w_r, b_r):
    t, d = h.shape
    bm = _pick(t, 256)
    vmem = 2 * (bm * d * 4 + d * LANES * 4) + 4 * bm * d * 4
    return pl.pallas_call(
        _router_kernel,
        out_shape=jax.ShapeDtypeStruct((t, LANES), F32),
        grid=(t // bm,),
        in_specs=[pl.BlockSpec((bm, d), lambda i: (i, 0)),
                  pl.BlockSpec((1, d), lambda i: (0, 0)),
                  pl.BlockSpec((d, LANES), lambda i: (0, 0)),
                  pl.BlockSpec((1, LANES), lambda i: (0, 0))],
        out_specs=pl.BlockSpec((bm, LANES), lambda i: (i, 0)),
        compiler_params=_params(("parallel",), vmem),
        name="moe_router",
    )(h, g, w_r, b_r)


def _expert_kernel(ue_ref, rows_ref, tok_ref, nused_ref, h_hbm, g_ref, wg_ref, wu_ref, wd_ref, out_ref,
                   xraw, xb, hg, hu, hb, sem):
    u = pl.program_id(0)
    c = pl.program_id(1)
    nc = xb.shape[0]
    nused = nused_ref[0]
    rows = xraw.shape[0]
    sub = MOE_SUB

    def padded(n):
        return pl.multiple_of(((n + 7) // 8) * 8, 8)

    def gather(unit):
        def body(r8, carry):
            for j in range(8):
                r = r8 * 8 + j
                t = tok_ref[unit * rows + r]
                pltpu.make_async_copy(h_hbm.at[pl.ds(t, 1), :], xraw.at[pl.ds(r, 1), :], sem.at[0]).start()
            return carry
        lax.fori_loop(0, padded(rows_ref[unit]) // 8, body, 0)

    @pl.when(u < nused)
    def _():
        nrows = rows_ref[u]

        @pl.when(c == 0)
        def _():
            @pl.when(u == 0)
            def _():
                xraw[...] = jnp.zeros(xraw.shape, F32)
                gather(0)

            n8 = padded(nrows)
            pltpu.make_async_copy(h_hbm.at[pl.ds(0, n8), :], xraw.at[pl.ds(0, n8), :], sem.at[0]).wait()
            kc = xb.shape[2]
            for s in range(rows // sub):
                sl = slice(s * sub, (s + 1) * sub)

                @pl.when(nrows > s * sub)
                def _():
                    x = xraw[sl, :]
                    xn = (x * lax.rsqrt(jnp.mean(x * x, axis=-1, keepdims=True) + EPS) * g_ref[...]).astype(BF16)
                    for j in range(nc):
                        xb[j, sl, :] = xn[:, j * kc:(j + 1) * kc]

            @pl.when(u + 1 < nused)
            def _():
                gather(u + 1)

        nslab = (nrows + sub - 1) // sub

        @pl.when(c < nc)
        def _():
            wg = wg_ref[0].astype(BF16)
            wu = wu_ref[0].astype(BF16)
            for k in range(1, rows // sub + 1):
                @pl.when(nslab == k)
                def _():
                    x = xb[c, 0:k * sub, :]
                    pg = jnp.dot(x, wg, preferred_element_type=F32)
                    pu = jnp.dot(x, wu, preferred_element_type=F32)

                    @pl.when(c == 0)
                    def _():
                        hg[0:k * sub, :] = pg
                        hu[0:k * sub, :] = pu

                    @pl.when(c > 0)
                    def _():
                        hg[0:k * sub, :] += pg
                        hu[0:k * sub, :] += pu

        @pl.when(c >= nc)
        def _():
            wd = wd_ref[0].astype(BF16)
            for k in range(1, rows // sub + 1):
                @pl.when(nslab == k)
                def _():
                    @pl.when(c == nc)
                    def _():
                        g = hg[0:k * sub, :]
                        hb[0:k * sub, :] = (g * (1.0 / (1.0 + jnp.exp(-g))) * hu[0:k * sub, :]).astype(BF16)

                    out_ref[0:k * sub, :] = jnp.dot(hb[0:k * sub, :], wd, preferred_element_type=F32)
                    if k * sub < rows:
                        out_ref[k * sub:, :] = jnp.zeros((rows - k * sub, out_ref.shape[1]), F32)


def _experts(h_all, g_ffn, w_gate, w_up, w_down, unit_expert, unit_rows, buf_tok, nused, n_units):
    t, d = h_all.shape
    de = w_gate.shape[2]
    rows = MOE_ROWS
    assert t >= rows
    nc = MOE_CHUNKS
    kc = d // nc

    def pos(u, c, nu):
        live = u < nu[0]
        return jnp.minimum(u, nu[0] - 1), jnp.where(live, c, 2 * nc - 1)

    def w_in_map(u, c, ue, rw, tok, nu):
        uu, cc = pos(u, c, nu)
        return (ue[uu], jnp.minimum(cc, nc - 1), 0)

    def w_down_map(u, c, ue, rw, tok, nu):
        uu, cc = pos(u, c, nu)
        return (ue[uu], 0, jnp.maximum(cc - nc, 0))

    def out_map(u, c, ue, rw, tok, nu):
        uu, cc = pos(u, c, nu)
        return (uu, jnp.maximum(cc - nc, 0))

    gs = pltpu.PrefetchScalarGridSpec(
        num_scalar_prefetch=4,
        grid=(n_units, 2 * nc),
        in_specs=[pl.BlockSpec(memory_space=pl.ANY),
                  pl.BlockSpec((1, d), lambda u, c, ue, rw, tok, nu: (0, 0)),
                  pl.BlockSpec((1, kc, de), w_in_map),
                  pl.BlockSpec((1, kc, de), w_in_map),
                  pl.BlockSpec((1, de, kc), w_down_map)],
        out_specs=pl.BlockSpec((rows, kc), out_map),
        scratch_shapes=[pltpu.VMEM((rows, d), F32), pltpu.VMEM((nc, rows, kc), BF16),
                        pltpu.VMEM((rows, de), F32), pltpu.VMEM((rows, de), F32),
                        pltpu.VMEM((rows, de), BF16), pltpu.SemaphoreType.DMA((1,))],
    )
    vmem = 2 * 3 * kc * de * 4 + 2 * rows * kc * 4 + rows * d * 6 + rows * de * 10 + 3 * kc * de * 2 \
        + rows * d * 4
    return pl.pallas_call(
        _expert_kernel,
        out_shape=jax.ShapeDtypeStruct((n_units * rows, d), F32),
        grid_spec=gs,
        compiler_params=_params(("arbitrary", "arbitrary"), vmem),
        name="moe_experts",
    )(unit_expert, unit_rows, buf_tok, nused, h_all, g_ffn, w_gate, w_up, w_down)


def _combine_kernel(slot_ref, h_ref, meta_ref, gfin_ref, eo_hbm, y_ref, obuf, sem, *, tile0):
    i = pl.program_id(0)
    n = pl.num_programs(0)
    rows = h_ref.shape[0]

    def gather(tile, buf):
        base = (tile0 + tile) * rows * 2

        def body(r8, carry):
            for j in range(8):
                r = r8 * 8 + j
                for k in range(2):
                    s = slot_ref[base + 2 * r + k]
                    pltpu.make_async_copy(eo_hbm.at[pl.ds(s, 1), :], obuf.at[buf, k, pl.ds(r, 1), :],
                                          sem.at[buf]).start()
            return carry
        lax.fori_loop(0, rows // 8, body, 0)

    buf = i % 2

    @pl.when(i == 0)
    def _():
        gather(0, 0)

    @pl.when(i + 1 < n)
    def _():
        gather(i + 1, 1 - buf)

    for k in range(2):
        pltpu.make_async_copy(eo_hbm.at[pl.ds(0, rows), :], obuf.at[buf, k], sem.at[buf]).wait()
    meta = meta_ref[...]
    y = meta[:, 2:3] * obuf[buf, 0] + meta[:, 3:4] * obuf[buf, 1]
    hh = h_ref[...] + y
    hn = hh * lax.rsqrt(jnp.mean(hh * hh, axis=-1, keepdims=True) + EPS)
    y_ref[...] = hn * gfin_ref[...]


def _combine(h_all, meta, g_final, expert_out, slots, tok0, n_tok):
    d = h_all.shape[1]
    rows = _pick(n_tok, COMBINE_ROWS)
    assert tok0 % rows == 0 and n_tok % rows == 0
    tile0 = tok0 // rows
    gs = pltpu.PrefetchScalarGridSpec(
        num_scalar_prefetch=1,
        grid=(n_tok // rows,),
        in_specs=[pl.BlockSpec((rows, d), lambda i, sl: (i + tile0, 0)),
                  pl.BlockSpec((rows, LANES), lambda i, sl: (i + tile0, 0)),
                  pl.BlockSpec((1, d), lambda i, sl: (0, 0)),
                  pl.BlockSpec(memory_space=pl.ANY)],
        out_specs=pl.BlockSpec((rows, d), lambda i, sl: (i, 0)),
        scratch_shapes=[pltpu.VMEM((2, 2, rows, d), F32), pltpu.SemaphoreType.DMA((2,))],
    )
    vmem = 4 * rows * d * 4 + 4 * rows * d * 4 + 4 * rows * d * 4
    return pl.pallas_call(
        functools.partial(_combine_kernel, tile0=tile0),
        out_shape=jax.ShapeDtypeStruct((n_tok, d), F32),
        grid_spec=gs,
        compiler_params=_params(("arbitrary",), vmem),
        name="moe_combine",
    )(slots, h_all, meta, g_final, expert_out)


def _dispatch_plan(meta, n_units):
    t = meta.shape[0]
    e_flat = meta[:, 0:2].astype(jnp.int32).reshape(-1)
    onehot = (e_flat[:, None] == jnp.arange(N_EXPERTS, dtype=jnp.int32)[None, :]).astype(jnp.int32)
    csum = jnp.cumsum(onehot, axis=0)
    rank = jnp.take_along_axis(csum, e_flat[:, None], axis=1)[:, 0] - 1
    counts = csum[-1]
    nunit = (counts + MOE_ROWS - 1) // MOE_ROWS
    unit_end = jnp.cumsum(nunit)
    unit_start = unit_end - nunit
    slots = unit_start[e_flat] * MOE_ROWS + rank
    nused = unit_end[-1]
    tok = jnp.arange(2 * t, dtype=jnp.int32) // 2
    buf_tok = jnp.zeros((n_units * MOE_ROWS,), jnp.int32).at[slots].set(tok)
    unit_ids = jnp.minimum(jnp.arange(n_units, dtype=jnp.int32), nused - 1)
    unit_expert = jnp.minimum(jnp.searchsorted(unit_end, unit_ids, side="right"), N_EXPERTS - 1)
    unit_rows = jnp.clip(counts[unit_expert] - (unit_ids - unit_start[unit_expert]) * MOE_ROWS, 0, MOE_ROWS)
    return (unit_expert.astype(jnp.int32), unit_rows.astype(jnp.int32), buf_tok,
            nused.reshape(1).astype(jnp.int32), slots.astype(jnp.int32))


def kernel(x_prompt, x_sample, cache_k, cache_v, state_conv, cache_mem_k, cache_mem_v, page_table, mem_prompt, g_mix, w_in, lambda_q1, lambda_k1, lambda_q2, lambda_k2, g_subln, w_conv, w_out, g_mem, w_ck, w_cv, g_cross, w_cq, w_co, g_ffn, w_router_group, b_router_group, w_router_expert, b_router_expert, w_e_gate, w_e_up, w_e_down, g_final):
    depth = g_mix.shape[0]
    assert depth == 1, "single-layer trunk only"
    batch, seq, d = x_prompt.shape
    bd, dec_seq, _ = x_sample.shape
    assert dec_seq == 1
    n_pool, page = cache_k.shape[1], cache_k.shape[2]
    n_heads = cache_k.shape[3]
    aw = n_heads * HEAD_DIM
    cw = w_conv.shape[2]
    cross_w = w_cq.shape[2]
    n_mem = mem_prompt.shape[1]
    past_len = page_table.shape[1] * page
    lam_init = 0.8 - 0.6 * math.exp(-0.3 * 0)
    tp, ts = batch * seq, bd

    w_in_b = w_in.reshape(w_in.shape[1:]).astype(BF16)
    w_out_b = w_out.reshape(w_out.shape[1:]).astype(BF16)
    w_ck_b, w_cv_b, w_cq_b, w_co_b = (w.reshape(w.shape[1:]).astype(BF16) for w in (w_ck, w_cv, w_cq, w_co))
    lams = tuple(v.reshape(1, QK_DIM) for v in (lambda_q1[0], lambda_k1[0], lambda_q2[0], lambda_k2[0]))
    gsub = g_subln[0].reshape(1, HEAD_DIM)
    attn_scale = QK_DIM ** -0.5
    cross_scale = (cross_w // N_CROSS_HEADS) ** -0.5
    col_q, col_k, col_v = 0, aw, 2 * aw
    col_b, col_c, col_x = 3 * aw, 3 * aw + cw, 3 * aw + 2 * cw

    def in_proj(a, rope, period):
        q = _matmul([a], w_in_b, col_q, aw, out_dtype=BF16, mode="rope", scale=attn_scale, rope=rope,
                    rope_period=period, name="in_proj_q")
        k = _matmul([a], w_in_b, col_k, aw, out_dtype=F32, mode="rope", rope=rope, rope_period=period,
                    name="in_proj_k")
        v = _matmul([a], w_in_b, col_v, aw, out_dtype=F32, name="in_proj_v")
        return q, k, v

    def after_mixer(x2d, o_attn, g_conv, mem_k, mem_v, cross_fn):
        h1 = _matmul([o_attn, g_conv], w_out_b, 0, d, out_dtype=F32, mode="residual", residual=x2d,
                     bn_pref=512, name="out_proj")
        a2 = _rmsnorm_bf16(h1, g_cross[0])
        qc = _matmul([a2], w_cq_b, 0, cross_w, out_dtype=BF16, scale=cross_scale, name="cross_q")
        oc = cross_fn(qc, mem_k, mem_v)
        return _matmul([oc], w_co_b, 0, d, out_dtype=F32, mode="residual", residual=h1, bn_pref=512,
                       name="cross_out")

    xp = x_prompt.reshape(tp, d)
    a_p = _rmsnorm_bf16(xp, g_mix[0])
    rope_p = _rope_tables(jnp.arange(seq, dtype=jnp.int32))
    q_p, k_p, v_p = in_proj(a_p, rope_p, seq)
    o_p = _causal_diff_attn(q_p.reshape(batch, seq, aw), k_p.reshape(batch, seq, aw),
                            v_p.reshape(batch, seq, aw), lams, gsub, lam_init).reshape(tp, aw)
    g_p, conv_p = _conv_branch_seq(a_p, w_in_b, col_b, col_c, col_x, w_conv[0], batch, seq)
    m_p = _rmsnorm_bf16(mem_prompt.reshape(batch * n_mem, d), g_mem[0])
    mem_k_p = _matmul([m_p], w_ck_b, 0, cross_w, out_dtype=F32, name="mem_k")
    mem_v_p = _matmul([m_p], w_cv_b, 0, cross_w, out_dtype=F32, name="mem_v")
    h2_p = after_mixer(
        xp, o_p, g_p, mem_k_p.reshape(batch, n_mem, cross_w), mem_v_p.reshape(batch, n_mem, cross_w),
        lambda qc, mk, mv: _cross_attn_seq(qc.reshape(batch, seq, cross_w), mk, mv).reshape(tp, cross_w))

    xs = x_sample.reshape(ts, d)
    a_s = _rmsnorm_bf16(xs, g_mix[0])
    rope_s = _rope_tables(jnp.full((ts,), past_len, dtype=jnp.int32))
    q_s, k_s, v_s = in_proj(a_s, rope_s, ts)
    hd3 = (ts, n_heads, HEAD_DIM)
    o_s = _paged_diff_attn(q_s.reshape(hd3), k_s.reshape(hd3), v_s.reshape(hd3),
                           cache_k.reshape(n_pool, page, n_heads, HEAD_DIM),
                           cache_v.reshape(n_pool, page, n_heads, HEAD_DIM),
                           page_table, lams, gsub, lam_init).reshape(ts, aw)
    st = state_conv[0]
    g_s, z_s = _conv_branch_step(a_s, w_in_b, col_b, col_c, col_x, w_conv[0], st[:, 0, :], st[:, 1, :])
    conv_s = jnp.stack([st[:, 1, :], z_s], axis=1)
    h2_s = after_mixer(xs, o_s, g_s, cache_mem_k.reshape(cache_mem_k.shape[1:]),
                       cache_mem_v.reshape(cache_mem_v.shape[1:]), _cross_attn_step)

    h_all = jnp.concatenate([h2_p, h2_s], axis=0)
    t_all = tp + ts
    w_r = jnp.concatenate([w_router_group[0], w_router_expert[0],
                           jnp.zeros((d, LANES - N_GROUPS - N_EXPERTS), F32)], axis=1)
    b_r = jnp.concatenate([b_router_group[0], b_router_expert[0],
                           jnp.zeros((LANES - N_GROUPS - N_EXPERTS,), F32)]).reshape(1, LANES)
    g_ffn2 = g_ffn[0].reshape(1, d)
    meta = _router(h_all, g_ffn2, w_r, b_r)
    n_units = (2 * t_all + N_EXPERTS * (MOE_ROWS - 1) + MOE_ROWS - 1) // MOE_ROWS
    unit_expert, unit_rows, buf_tok, nused, slots = _dispatch_plan(meta, n_units)
    expert_out = _experts(h_all, g_ffn2, w_e_gate.reshape(w_e_gate.shape[1:]), w_e_up.reshape(w_e_up.shape[1:]),
                          w_e_down.reshape(w_e_down.shape[1:]), unit_expert, unit_rows, buf_tok, nused,
                          n_units)
    g_fin = g_final.reshape(1, d)
    y_p = _combine(h_all, meta, g_fin, expert_out, slots, 0, tp)
    y_s = _combine(h_all, meta, g_fin, expert_out, slots, tp, ts)

    return (y_p.reshape(batch, seq, d), y_s.reshape(bd, 1, d),
            k_p.reshape(1, batch, seq, n_heads, HEAD_DIM), v_p.reshape(1, batch, seq, n_heads, HEAD_DIM),
            conv_p.reshape(1, batch, CONV_K - 1, cw),
            mem_k_p.reshape(1, batch, n_mem, N_CROSS_HEADS, cross_w // N_CROSS_HEADS),
            mem_v_p.reshape(1, batch, n_mem, N_CROSS_HEADS, cross_w // N_CROSS_HEADS),
            k_s.reshape(1, bd, 1, n_heads, HEAD_DIM), v_s.reshape(1, bd, 1, n_heads, HEAD_DIM),
            conv_s.reshape(1, bd, CONV_K - 1, cw))
```

```python
import functools
import math

import jax
import jax.numpy as jnp
from jax import lax
from jax.experimental import pallas as pl
from jax.experimental.pallas import tpu as pltpu

F32 = jnp.float32
BF16 = jnp.bfloat16

EPS = 1e-6
HEAD_DIM = 128
QK_DIM = HEAD_DIM // 2
ROPE_DIM = QK_DIM // 4
ROPE_HALF = ROPE_DIM // 2
ROPE_THETA = 500000.0
N_CROSS_HEADS = 4
N_GROUPS = 8
EXPERTS_PER_GROUP = 8
N_EXPERTS = N_GROUPS * EXPERTS_PER_GROUP
CONV_K = 3

LANES = 128
V7X_VMEM_BYTES = 64 * 1024 * 1024
VMEM_CAP = V7X_VMEM_BYTES - 6 * 1024 * 1024
MOE_ROWS = 384
MOE_SUB = 128
MOE_CHUNKS = 4
MOE_RING_IN = 8
MOE_RING_DN = 4
COMBINE_ROWS = 128


def _pick(n, pref):
    if n <= pref:
        return n
    for d in range(pref, 7, -1):
        if n % d == 0 and d % 8 == 0:
            return d
    return n


def _params(sem, vmem_est):
    limit = int(min(VMEM_CAP, max(32 * 1024 * 1024, vmem_est * 5 // 4 + (2 << 20))))
    return pltpu.CompilerParams(dimension_semantics=sem, vmem_limit_bytes=limit)


def _nt_dot(a, b):
    return lax.dot_general(a, b, (((1,), (1,)), ((), ())), preferred_element_type=F32)


def _rmsnorm_kernel(x_ref, g_ref, o_ref):
    x = x_ref[...]
    y = x * lax.rsqrt(jnp.mean(x * x, axis=-1, keepdims=True) + EPS)
    o_ref[...] = (y * g_ref[...]).astype(o_ref.dtype)


def _rmsnorm_bf16(x, g):
    m, d = x.shape
    bm = _pick(m, 256)
    return pl.pallas_call(
        _rmsnorm_kernel,
        out_shape=jax.ShapeDtypeStruct((m, d), BF16),
        grid=(m // bm,),
        in_specs=[pl.BlockSpec((bm, d), lambda i: (i, 0)),
                  pl.BlockSpec((1, d), lambda i: (0, 0))],
        out_specs=pl.BlockSpec((bm, d), lambda i: (i, 0)),
        compiler_params=_params(("parallel",), 2 * bm * d * 6),
        name="rmsnorm_bf16",
    )(x, g.reshape(1, d))


def _mm_kernel(*refs, k_splits, mode, scale):
    n_lhs = len(k_splits)
    lhs = refs[:n_lhs]
    w_ref = refs[n_lhs]
    extra = refs[n_lhs + 1:-1]
    o_ref = refs[-1]
    acc = None
    k0 = 0
    for a_ref, kk in zip(lhs, k_splits):
        part = jnp.dot(a_ref[...], w_ref[k0:k0 + kk, :], preferred_element_type=F32)
        acc = part if acc is None else acc + part
        k0 += kk
    if mode == "rope":
        cos_ref, sa_ref, sb_ref = extra
        cos, sa, sb = cos_ref[...], sa_ref[...], sb_ref[...]
        for c in range(acc.shape[1] // LANES):
            blk = acc[:, c * LANES:(c + 1) * LANES]
            up = pltpu.roll(blk, LANES - ROPE_HALF, axis=1)
            dn = pltpu.roll(blk, ROPE_HALF, axis=1)
            rot = blk * cos + up * sa + dn * sb
            if scale != 1.0:
                rot = rot * scale
            o_ref[:, c * LANES:(c + 1) * LANES] = rot.astype(o_ref.dtype)
        return
    if mode == "residual":
        acc = extra[0][...] + acc
    if scale != 1.0:
        acc = acc * scale
    o_ref[...] = acc.astype(o_ref.dtype)


def _matmul(lhs_list, w, col0, n, *, out_dtype, mode="plain", scale=1.0, rope=None,
            rope_period=None, residual=None, bm_pref=1024, bn_pref=1024, name="matmul"):
    m = lhs_list[0].shape[0]
    k_splits = tuple(a.shape[1] for a in lhs_list)
    k_total = sum(k_splits)
    assert w.shape[0] == k_total
    bm = _pick(m, bm_pref)
    bn = _pick(n, bn_pref)
    assert col0 % bn == 0 and n % bn == 0 and m % bm == 0
    joff = col0 // bn
    in_specs = [pl.BlockSpec((bm, kk), lambda i, j: (i, 0)) for kk in k_splits]
    in_specs.append(pl.BlockSpec((k_total, bn), lambda i, j: (0, j + joff)))
    args = list(lhs_list) + [w]
    vmem = 2 * (bm * k_total * 2 + k_total * bn * 2 + bm * bn * 4) + 2 * bm * bn * 4
    if mode == "rope":
        assert rope_period % bm == 0 or bm % rope_period == 0
        if bm > rope_period:
            rope = tuple(jnp.tile(t, (bm // rope_period, 1)) for t in rope)
            nper = 1
        else:
            nper = rope_period // bm
        in_specs += [pl.BlockSpec((bm, LANES), lambda i, j: (i % nper, 0))] * 3
        args += list(rope)
    elif mode == "residual":
        in_specs.append(pl.BlockSpec((bm, bn), lambda i, j: (i, j)))
        args.append(residual)
        vmem += 2 * bm * bn * 4
    return pl.pallas_call(
        functools.partial(_mm_kernel, k_splits=k_splits, mode=mode, scale=scale),
        out_shape=jax.ShapeDtypeStruct((m, n), out_dtype),
        grid=(m // bm, n // bn),
        in_specs=in_specs,
        out_specs=pl.BlockSpec((bm, bn), lambda i, j: (i, j)),
        compiler_params=_params(("parallel", "parallel"), vmem),
        name=name,
    )(*args)


def _rope_tables(pos):
    inv_freq = ROPE_THETA ** (-jnp.arange(ROPE_HALF, dtype=F32) * (2.0 / ROPE_DIM))
    ang = pos.astype(F32)[:, None] * inv_freq[None, :]
    cos, sin = jnp.cos(ang), jnp.sin(ang)
    n = pos.shape[0]
    one = jnp.ones((n, QK_DIM - ROPE_DIM), F32)
    zero = jnp.zeros((n, QK_DIM - ROPE_DIM), F32)
    z8 = jnp.zeros((n, ROPE_HALF), F32)
    cos64 = jnp.concatenate([cos, cos, one], axis=1)
    up64 = jnp.concatenate([-sin, z8, zero], axis=1)
    dn64 = jnp.concatenate([z8, sin, zero], axis=1)
    rep = LANES // QK_DIM
    return tuple(jnp.tile(t, (1, rep)) for t in (cos64, up64, dn64))


def _conv_seq_kernel(a_ref, wb_ref, wc_ref, wx_ref, wconv_ref, g_ref, nc_ref, zbuf, *, tiles_per_seq):
    i = pl.program_id(1)
    bm = a_ref.shape[0]
    a = a_ref[...]
    cb = jnp.dot(a, wb_ref[...], preferred_element_type=F32)
    cc = jnp.dot(a, wc_ref[...], preferred_element_type=F32)
    cx = jnp.dot(a, wx_ref[...], preferred_element_type=F32)
    z = cc * cx
    first = (i % tiles_per_seq) == 0

    @pl.when(first)
    def _():
        zbuf[0:8, :] = jnp.zeros((8, zbuf.shape[1]), F32)

    @pl.when(jnp.logical_not(first))
    def _():
        zbuf[0:8, :] = zbuf[bm:bm + 8, :]

    zbuf[8:8 + bm, :] = z
    w = wconv_ref[...]
    y = w[0:1, :] * zbuf[6:6 + bm, :] + w[1:2, :] * zbuf[7:7 + bm, :]
    y = y + w[2:3, :] * z
    g_ref[...] = (cb * y).astype(g_ref.dtype)

    @pl.when((i % tiles_per_seq) == tiles_per_seq - 1)
    def _():
        nc_ref[0] = z[bm - (CONV_K - 1):, :]


def _conv_branch_seq(a, w, col_b, col_c, col_x, w_conv, batch, seq):
    m, k = a.shape
    cw = w_conv.shape[1]
    assert seq >= 8
    bm = _pick(seq, 512)
    bc = _pick(cw, 512)
    tps = seq // bm
    ob, oc, ox = col_b // bc, col_c // bc, col_x // bc
    vmem = 2 * (bm * k * 2 + 3 * k * bc * 2 + bm * bc * 2) + 6 * bm * bc * 4
    return pl.pallas_call(
        functools.partial(_conv_seq_kernel, tiles_per_seq=tps),
        out_shape=(jax.ShapeDtypeStruct((m, cw), BF16),
                   jax.ShapeDtypeStruct((batch, CONV_K - 1, cw), F32)),
        grid=(cw // bc, m // bm),
        in_specs=[pl.BlockSpec((bm, k), lambda c, i: (i, 0)),
                  pl.BlockSpec((k, bc), lambda c, i: (0, ob + c)),
                  pl.BlockSpec((k, bc), lambda c, i: (0, oc + c)),
                  pl.BlockSpec((k, bc), lambda c, i: (0, ox + c)),
                  pl.BlockSpec((CONV_K, bc), lambda c, i: (0, c))],
        out_specs=(pl.BlockSpec((bm, bc), lambda c, i: (i, c)),
                   pl.BlockSpec((1, CONV_K - 1, bc), lambda c, i: (i // tps, 0, c))),
        scratch_shapes=[pltpu.VMEM((bm + 8, bc), F32)],
        compiler_params=_params(("parallel", "arbitrary"), vmem),
        name="conv_branch_seq",
    )(a, w, w, w, w_conv)


def _conv_step_kernel(a_ref, wb_ref, wc_ref, wx_ref, wconv_ref, s0_ref, s1_ref, g_ref, z_ref):
    a = a_ref[...]
    cb = jnp.dot(a, wb_ref[...], preferred_element_type=F32)
    cc = jnp.dot(a, wc_ref[...], preferred_element_type=F32)
    cx = jnp.dot(a, wx_ref[...], preferred_element_type=F32)
    z = cc * cx
    w = wconv_ref[...]
    y = w[0:1, :] * s0_ref[...] + w[1:2, :] * s1_ref[...]
    y = y + w[2:3, :] * z
    g_ref[...] = (cb * y).astype(g_ref.dtype)
    z_ref[...] = z


def _conv_branch_step(a, w, col_b, col_c, col_x, w_conv, s0, s1):
    m, k = a.shape
    cw = w_conv.shape[1]
    bc = _pick(cw, 512)
    ob, oc, ox = col_b // bc, col_c // bc, col_x // bc
    vmem = 2 * (m * k * 2 + 3 * k * bc * 2 + 5 * m * bc * 4)
    return pl.pallas_call(
        _conv_step_kernel,
        out_shape=(jax.ShapeDtypeStruct((m, cw), BF16), jax.ShapeDtypeStruct((m, cw), F32)),
        grid=(cw // bc,),
        in_specs=[pl.BlockSpec((m, k), lambda c: (0, 0)),
                  pl.BlockSpec((k, bc), lambda c: (0, ob + c)),
                  pl.BlockSpec((k, bc), lambda c: (0, oc + c)),
                  pl.BlockSpec((k, bc), lambda c: (0, ox + c)),
                  pl.BlockSpec((CONV_K, bc), lambda c: (0, c)),
                  pl.BlockSpec((m, bc), lambda c: (0, c)),
                  pl.BlockSpec((m, bc), lambda c: (0, c))],
        out_specs=(pl.BlockSpec((m, bc), lambda c: (0, c)),
                   pl.BlockSpec((m, bc), lambda c: (0, c))),
        compiler_params=_params(("parallel",), vmem),
        name="conv_branch_step",
    )(a, w, w, w, w_conv, s0, s1)


def _diff_lambda(lq1_ref, lk1_ref, lq2_ref, lk2_ref, lam_init):
    a = jnp.sum(lq1_ref[...] * lk1_ref[...], axis=-1, keepdims=True)
    b = jnp.sum(lq2_ref[...] * lk2_ref[...], axis=-1, keepdims=True)
    return jnp.exp(a) - jnp.exp(b) + lam_init


def _subln(d, gsub, lam_init):
    dn = d * lax.rsqrt(jnp.mean(d * d, axis=-1, keepdims=True) + EPS)
    return dn * gsub * (1.0 - lam_init)


def _causal_attn_kernel(q_ref, k_ref, v_ref, lq1, lk1, lq2, lk2, gsub_ref, o_ref, kb, vb, *, bq, lam_init):
    seq = q_ref.shape[1]
    kb[...] = k_ref[0].astype(BF16)
    vb[...] = v_ref[0].astype(BF16)
    lam = _diff_lambda(lq1, lk1, lq2, lk2, lam_init)
    gsub = gsub_ref[...]
    lane = lax.broadcasted_iota(jnp.int32, (bq, HEAD_DIM), 1)
    row = lax.broadcasted_iota(jnp.int32, (2 * bq, bq), 0)
    col = lax.broadcasted_iota(jnp.int32, (2 * bq, bq), 1)
    tri = col <= jnp.where(row >= bq, row - bq, row)
    for qi in range(seq // bq):
        q = q_ref[0, qi * bq:(qi + 1) * bq, :]
        zero = jnp.zeros_like(q)
        qs = jnp.concatenate([jnp.where(lane < QK_DIM, q, zero), jnp.where(lane >= QK_DIM, q, zero)], axis=0)
        past = qi * bq
        s_diag = jnp.where(tri, _nt_dot(qs, kb[past:past + bq, :]), -jnp.inf)
        m = jnp.max(s_diag, axis=-1, keepdims=True)
        if past:
            s_past = _nt_dot(qs, kb[0:past, :])
            m = jnp.maximum(m, jnp.max(s_past, axis=-1, keepdims=True))
        p_diag = jnp.exp(s_diag - m)
        l = jnp.sum(p_diag, axis=-1, keepdims=True)
        o = jnp.dot(p_diag.astype(BF16), vb[past:past + bq, :], preferred_element_type=F32)
        if past:
            p_past = jnp.exp(s_past - m)
            l = l + jnp.sum(p_past, axis=-1, keepdims=True)
            o = o + jnp.dot(p_past.astype(BF16), vb[0:past, :], preferred_element_type=F32)
        o = o / l
        d = o[:bq] - lam * o[bq:]
        o_ref[0, qi * bq:(qi + 1) * bq, :] = _subln(d, gsub, lam_init).astype(o_ref.dtype)


def _causal_diff_attn(q, k, v, lams, gsub, lam_init):
    b, s, w = q.shape
    h = w // HEAD_DIM
    bq = _pick(s, 256)
    blk = lambda: pl.BlockSpec((1, s, HEAD_DIM), lambda bi, hi: (bi, 0, hi))
    small = lambda n: pl.BlockSpec((1, n), lambda bi, hi: (0, 0))
    vmem = 2 * s * HEAD_DIM * (2 + 4 + 4 + 2) + 2 * s * HEAD_DIM * 2 + 6 * 2 * bq * s * 4
    return pl.pallas_call(
        functools.partial(_causal_attn_kernel, bq=bq, lam_init=lam_init),
        out_shape=jax.ShapeDtypeStruct((b, s, w), BF16),
        grid=(b, h),
        in_specs=[blk(), blk(), blk(), small(QK_DIM), small(QK_DIM), small(QK_DIM), small(QK_DIM),
                  small(HEAD_DIM)],
        out_specs=blk(),
        scratch_shapes=[pltpu.VMEM((s, HEAD_DIM), BF16), pltpu.VMEM((s, HEAD_DIM), BF16)],
        compiler_params=_params(("parallel", "parallel"), vmem),
        name="causal_diff_attn",
    )(q, k, v, *lams, gsub)


def _paged_attn_kernel(pt_ref, q_ref, kc_ref, vc_ref, bias_ref, *rest, n_heads, ppb, lam_init):
    kp_refs, vp_refs = rest[:ppb], rest[ppb:2 * ppb]
    lq1, lk1, lq2, lk2, gsub_ref, o_ref, qm, m_s, l_s, acc = rest[2 * ppb:]
    p = pl.program_id(1)
    page = kp_refs[0].shape[1]

    @pl.when(p == 0)
    def _():
        q = q_ref[0].astype(F32)
        lane = lax.broadcasted_iota(jnp.int32, q.shape, 1)
        qm[0:n_heads, :] = jnp.where(lane < QK_DIM, q, 0.0).astype(BF16)
        qm[n_heads:, :] = jnp.where(lane >= QK_DIM, q, 0.0).astype(BF16)
        m_s[...] = jnp.full(m_s.shape, -jnp.inf, F32)
        l_s[...] = jnp.zeros(l_s.shape, F32)
        acc[...] = jnp.zeros(acc.shape, F32)

    qv = qm[...]
    bias = bias_ref[...]
    ss = [_nt_dot(qv, kp_refs[j][0].reshape(page * n_heads, HEAD_DIM).astype(BF16)) + bias
          for j in range(ppb)]
    m_old = m_s[...]
    m_new = m_old
    for s in ss:
        m_new = jnp.maximum(m_new, jnp.max(s, axis=-1, keepdims=True))
    corr = jnp.exp(m_old - m_new)
    l = l_s[...] * corr
    a = acc[...] * corr
    for j, s in enumerate(ss):
        pr = jnp.exp(s - m_new)
        l = l + jnp.sum(pr, axis=-1, keepdims=True)
        a = a + jnp.dot(pr.astype(BF16), vp_refs[j][0].reshape(page * n_heads, HEAD_DIM).astype(BF16),
                        preferred_element_type=F32)
    l_s[...] = l
    acc[...] = a
    m_s[...] = m_new

    @pl.when(p == pl.num_programs(1) - 1)
    def _():
        kc = kc_ref[0].astype(BF16).astype(F32)
        vc = vc_ref[0].astype(BF16).astype(F32)
        kc2 = jnp.concatenate([kc, kc], axis=0)
        vc2 = jnp.concatenate([vc, vc], axis=0)
        s_c = jnp.sum(qm[...].astype(F32) * kc2, axis=-1, keepdims=True)
        m_old = m_s[...]
        m_new = jnp.maximum(m_old, s_c)
        corr = jnp.exp(m_old - m_new)
        p_c = jnp.exp(s_c - m_new)
        l = l_s[...] * corr + p_c
        a = acc[...] * corr + p_c.astype(BF16).astype(F32) * vc2
        o = a / l
        lam = _diff_lambda(lq1, lk1, lq2, lk2, lam_init)
        d = o[:n_heads] - lam * o[n_heads:]
        o_ref[0] = _subln(d, gsub_ref[...], lam_init).astype(o_ref.dtype)


def _paged_diff_attn(q, k_cur, v_cur, cache_k, cache_v, page_table, lams, gsub, lam_init):
    bd, n_heads, _ = q.shape
    n_pages = page_table.shape[1]
    page = cache_k.shape[1]
    ppb = 4 if n_pages % 4 == 0 else (2 if n_pages % 2 == 0 else 1)
    nrow = 2 * n_heads
    ncol = page * n_heads
    own = ((jnp.arange(ncol, dtype=jnp.int32)[None, :] % n_heads)
           == (jnp.arange(nrow, dtype=jnp.int32)[:, None] % n_heads))
    bias = jnp.where(own, 0.0, -jnp.inf).astype(F32)
    row = lambda: pl.BlockSpec((1, n_heads, HEAD_DIM), lambda b, p, pt: (b, 0, 0))

    def pg(j):
        return pl.BlockSpec((1, page, n_heads, HEAD_DIM),
                            lambda b, p, pt: (pt[b * n_pages + p * ppb + j], 0, 0, 0))

    small = lambda n: pl.BlockSpec((1, n), lambda b, p, pt: (0, 0))
    gs = pltpu.PrefetchScalarGridSpec(
        num_scalar_prefetch=1,
        grid=(bd, n_pages // ppb),
        in_specs=[row(), row(), row(), pl.BlockSpec((nrow, ncol), lambda b, p, pt: (0, 0))]
        + [pg(j) for j in range(ppb)] + [pg(j) for j in range(ppb)]
        + [small(QK_DIM), small(QK_DIM), small(QK_DIM), small(QK_DIM), small(HEAD_DIM)],
        out_specs=row(),
        scratch_shapes=[pltpu.VMEM((nrow, HEAD_DIM), BF16), pltpu.VMEM((nrow, 1), F32),
                        pltpu.VMEM((nrow, 1), F32), pltpu.VMEM((nrow, HEAD_DIM), F32)],
    )
    vmem = 4 * ppb * ncol * HEAD_DIM * 4 + 2 * ppb * ncol * HEAD_DIM * 2 + (4 + 3 * ppb) * nrow * ncol * 4
    return pl.pallas_call(
        functools.partial(_paged_attn_kernel, n_heads=n_heads, ppb=ppb, lam_init=lam_init),
        out_shape=jax.ShapeDtypeStruct((bd, n_heads, HEAD_DIM), BF16),
        grid_spec=gs,
        compiler_params=_params(("parallel", "arbitrary"), vmem),
        name="paged_diff_attn",
    )(page_table.reshape(-1), q, k_cur, v_cur, bias, *([cache_k] * ppb), *([cache_v] * ppb), *lams, gsub)


def _cross_attn_kernel(q_ref, mk_ref, mv_ref, o_ref, mkb, mvb):
    @pl.when(pl.program_id(1) == 0)
    def _():
        mkb[...] = mk_ref[0].astype(BF16)
        mvb[...] = mv_ref[0].astype(BF16)

    hd = q_ref.shape[2] // N_CROSS_HEADS
    for h in range(N_CROSS_HEADS):
        sl = slice(h * hd, (h + 1) * hd)
        s = _nt_dot(q_ref[0, :, sl], mkb[:, sl])
        p = jnp.exp(s - jnp.max(s, axis=-1, keepdims=True))
        l = jnp.sum(p, axis=-1, keepdims=True)
        o = jnp.dot(p.astype(BF16), mvb[:, sl], preferred_element_type=F32) / l
        o_ref[0, :, sl] = o.astype(o_ref.dtype)


def _cross_attn_seq(q, mem_k, mem_v):
    b, s, cw = q.shape
    n_mem = mem_k.shape[1]
    bq = _pick(s, 512)
    vmem = 2 * (2 * bq * cw * 2 + 2 * n_mem * cw * 4) + 2 * n_mem * cw * 2 + 6 * bq * n_mem * 4
    return pl.pallas_call(
        _cross_attn_kernel,
        out_shape=jax.ShapeDtypeStruct((b, s, cw), BF16),
        grid=(b, s // bq),
        in_specs=[pl.BlockSpec((1, bq, cw), lambda bi, i: (bi, i, 0)),
                  pl.BlockSpec((1, n_mem, cw), lambda bi, i: (bi, 0, 0)),
                  pl.BlockSpec((1, n_mem, cw), lambda bi, i: (bi, 0, 0))],
        out_specs=pl.BlockSpec((1, bq, cw), lambda bi, i: (bi, i, 0)),
        scratch_shapes=[pltpu.VMEM((n_mem, cw), BF16), pltpu.VMEM((n_mem, cw), BF16)],
        compiler_params=_params(("parallel", "arbitrary"), vmem),
        name="cross_attn_seq",
    )(q, mem_k, mem_v)


def _cross_step_kernel(q_ref, mk_hbm, mv_hbm, o_ref, kbuf, vbuf, sem):
    i = pl.program_id(0)
    n = pl.num_programs(0)
    bb, _, cw = q_ref.shape
    nh, hd = kbuf.shape[1], kbuf.shape[4]

    def copies(step, slot):
        out = []
        for h in range(nh):
            src = pl.ds(step * bb, bb)
            out.append(pltpu.make_async_copy(mk_hbm.at[src, :, h, :], kbuf.at[slot, h], sem.at[slot, 0]))
            out.append(pltpu.make_async_copy(mv_hbm.at[src, :, h, :], vbuf.at[slot, h], sem.at[slot, 1]))
        return out

    slot = i % 2

    @pl.when(i == 0)
    def _():
        for c in copies(0, 0):
            c.start()

    @pl.when(i + 1 < n)
    def _():
        for c in copies(i + 1, 1 - slot):
            c.start()

    for c in copies(i, slot):
        c.wait()
    for j in range(bb):
        for h in range(nh):
            sl = slice(h * hd, (h + 1) * hd)
            qh = jnp.broadcast_to(q_ref[j, :, sl], (8, hd))
            s = _nt_dot(qh, kbuf[slot, h, j].astype(BF16))
            p = jnp.exp(s - jnp.max(s, axis=-1, keepdims=True))
            l = jnp.sum(p, axis=-1, keepdims=True)
            o = jnp.dot(p.astype(BF16), vbuf[slot, h, j].astype(BF16), preferred_element_type=F32) / l
            o_ref[j, :, sl] = o[0:1].astype(o_ref.dtype)


def _cross_attn_step(q, mem_k, mem_v):
    bd, cw = q.shape
    n_mem, nh, hd = mem_k.shape[1:]
    assert nh == N_CROSS_HEADS and nh * hd == cw
    bb = 4 if bd % 4 == 0 else 1
    buf = pltpu.VMEM((2, nh, bb, n_mem, hd), F32)
    vmem = 2 * 2 * nh * bb * n_mem * hd * 4 + 4 * n_mem * hd * 4
    out = pl.pallas_call(
        _cross_step_kernel,
        out_shape=jax.ShapeDtypeStruct((bd, 1, cw), BF16),
        grid=(bd // bb,),
        in_specs=[pl.BlockSpec((bb, 1, cw), lambda i: (i, 0, 0)),
                  pl.BlockSpec(memory_space=pl.ANY), pl.BlockSpec(memory_space=pl.ANY)],
        out_specs=pl.BlockSpec((bb, 1, cw), lambda i: (i, 0, 0)),
        scratch_shapes=[buf, buf, pltpu.SemaphoreType.DMA((2, 2))],
        compiler_params=_params(("arbitrary",), vmem),
        name="cross_attn_step",
    )(q.reshape(bd, 1, cw), mem_k, mem_v)
    return out.reshape(bd, cw)


def _router_kernel(h_ref, g_ref, whi_ref, wlo_ref, br_ref, meta_ref):
    x = h_ref[...]
    xn = x * lax.rsqrt(jnp.mean(x * x, axis=-1, keepdims=True) + EPS) * g_ref[...]
    hi = xn.astype(BF16)
    lo = (xn - hi.astype(F32)).astype(BF16)
    whi = whi_ref[...]
    logits = (jnp.dot(hi, whi, preferred_element_type=F32)
              + (jnp.dot(hi, wlo_ref[...], preferred_element_type=F32)
                 + jnp.dot(lo, whi, preferred_element_type=F32)))
    logits = logits + br_ref[...]
    bm = logits.shape[0]
    lane = lax.broadcasted_iota(jnp.int32, (bm, LANES), 1)
    lanef = lane.astype(F32)
    neg = jnp.float32(-jnp.inf)
    big = jnp.float32(LANES)

    def top(mask):
        v = jnp.max(jnp.where(mask, logits, neg), axis=-1, keepdims=True)
        idx = jnp.min(jnp.where(jnp.logical_and(mask, logits == v), lanef, big), axis=-1, keepdims=True)
        return v, idx

    gmask = lane < N_GROUPS
    gmax, gidx = top(gmask)
    gsum = jnp.sum(jnp.where(gmask, jnp.exp(logits - gmax), 0.0), axis=-1, keepdims=True)
    g_top_p = 1.0 / gsum
    lo = N_GROUPS + gidx * EXPERTS_PER_GROUP
    emask = jnp.logical_and(lanef >= lo, lanef < lo + EXPERTS_PER_GROUP)
    v1, i1 = top(emask)
    v2, i2 = top(jnp.logical_and(emask, lanef != i1))
    r = jnp.exp(v2 - v1)
    den = 1.0 + r
    gate1 = g_top_p * (1.0 / den)
    gate2 = g_top_p * (r / den)
    meta = jnp.where(lane == 0, i1 - N_GROUPS,
                     jnp.where(lane == 1, i2 - N_GROUPS,
                               jnp.where(lane == 2, gate1, jnp.where(lane == 3, gate2, 0.0))))
    meta_ref[...] = meta


def _router(h, g, w_r, b_r):
    t, d = h.shape
    bm = _pick(t, 256)
    w_hi = w_r.astype(BF16)
    w_lo = (w_r - w_hi.astype(F32)).astype(BF16)
    vmem = 2 * (bm * d * 4 + 2 * d * LANES * 2) + 5 * bm * d * 4
    return pl.pallas_call(
        _router_kernel,
        out_shape=jax.ShapeDtypeStruct((t, LANES), F32),
        grid=(t // bm,),
        in_specs=[pl.BlockSpec((bm, d), lambda i: (i, 0)),
                  pl.BlockSpec((1, d), lambda i: (0, 0)),
                  pl.BlockSpec((d, LANES), lambda i: (0, 0)),
                  pl.BlockSpec((d, LANES), lambda i: (0, 0)),
                  pl.BlockSpec((1, LANES), lambda i: (0, 0))],
        out_specs=pl.BlockSpec((bm, LANES), lambda i: (i, 0)),
        compiler_params=_params(("parallel",), vmem),
        name="moe_router",
    )(h, g, w_hi, w_lo, b_r)


def _expert_kernel(ue_ref, rows_ref, tok_ref, nused_ref, h_hbm, hs_hbm, g_ref, wg_hbm, wu_hbm, wd_hbm, out_ref,
                   xraw, xb, hg, hu, hb, ring_in, ring_dn, sem_x, sem_in, sem_dn):
    u = pl.program_id(0)
    nc = xb.shape[0]
    kc = xb.shape[2]
    nused = nused_ref[0]
    rows = xraw.shape[0]
    sub = MOE_SUB
    tp = h_hbm.shape[0]
    n_in, n_dn = ring_in.shape[0], ring_dn.shape[0]

    def padded(n):
        return pl.multiple_of(((n + 7) // 8) * 8, 8)

    def gather(unit):
        def body(r8, carry):
            for j in range(8):
                r = r8 * 8 + j
                t = tok_ref[unit * rows + r]
                dst = xraw.at[pl.ds(r, 1), :]

                @pl.when(t < tp)
                def _():
                    pltpu.make_async_copy(h_hbm.at[pl.ds(t, 1), :], dst, sem_x.at[0]).start()

                @pl.when(t >= tp)
                def _():
                    pltpu.make_async_copy(hs_hbm.at[pl.ds(t - tp, 1), :], dst, sem_x.at[0]).start()
            return carry
        lax.fori_loop(0, padded(rows_ref[unit]) // 8, body, 0)

    def in_copy(unit, q):
        src = wg_hbm if q % 2 == 0 else wu_hbm
        slot = q % n_in
        return pltpu.make_async_copy(src.at[ue_ref[unit], pl.ds((q // 2) * kc, kc), :], ring_in.at[slot],
                                     sem_in.at[slot])

    def dn_copy(unit, q):
        slot = q % n_dn
        return pltpu.make_async_copy(wd_hbm.at[ue_ref[unit], :, pl.ds(q * kc, kc)], ring_dn.at[slot],
                                     sem_dn.at[slot])

    def refill(copy_fn, q, n_slots, n_chunks):
        nxt = q + n_slots
        if nxt < n_chunks:
            copy_fn(u, nxt).start()
        else:
            @pl.when(u + 1 < nused)
            def _():
                copy_fn(u + 1, nxt - n_chunks).start()

    @pl.when(u < nused)
    def _():
        nrows = rows_ref[u]
        nslab = (nrows + sub - 1) // sub

        @pl.when(u == 0)
        def _():
            xraw[...] = jnp.zeros(xraw.shape, F32)
            gather(0)
            for q in range(n_in):
                in_copy(0, q).start()
            for q in range(n_dn):
                dn_copy(0, q).start()

        n8 = padded(nrows)
        pltpu.make_async_copy(h_hbm.at[pl.ds(0, n8), :], xraw.at[pl.ds(0, n8), :], sem_x.at[0]).wait()
        for s in range(rows // sub):
            sl = slice(s * sub, (s + 1) * sub)

            @pl.when(nrows > s * sub)
            def _():
                x = xraw[sl, :]
                xn = (x * lax.rsqrt(jnp.mean(x * x, axis=-1, keepdims=True) + EPS) * g_ref[...]).astype(BF16)
                for j in range(nc):
                    xb[j, sl, :] = xn[:, j * kc:(j + 1) * kc]

        @pl.when(u + 1 < nused)
        def _():
            gather(u + 1)

        for c in range(nc):
            in_copy(u, 2 * c).wait()
            in_copy(u, 2 * c + 1).wait()
            wg = ring_in[(2 * c) % n_in].astype(BF16)
            wu = ring_in[(2 * c + 1) % n_in].astype(BF16)
            for k in range(1, rows // sub + 1):
                @pl.when(nslab == k)
                def _():
                    x = xb[c, 0:k * sub, :]
                    pg = jnp.dot(x, wg, preferred_element_type=F32)
                    pu = jnp.dot(x, wu, preferred_element_type=F32)
                    if c == 0:
                        hg[0:k * sub, :] = pg
                        hu[0:k * sub, :] = pu
                    else:
                        hg[0:k * sub, :] += pg
                        hu[0:k * sub, :] += pu
            refill(in_copy, 2 * c, n_in, 2 * nc)
            refill(in_copy, 2 * c + 1, n_in, 2 * nc)

        for k in range(1, rows // sub + 1):
            @pl.when(nslab == k)
            def _():
                g = hg[0:k * sub, :]
                hb[0:k * sub, :] = (g * (1.0 / (1.0 + jnp.exp(-g))) * hu[0:k * sub, :]).astype(BF16)

        for c in range(nc):
            dn_copy(u, c).wait()
            wd = ring_dn[c % n_dn].astype(BF16)
            cols = slice(c * kc, (c + 1) * kc)
            for k in range(1, rows // sub + 1):
                @pl.when(nslab == k)
                def _():
                    out_ref[0:k * sub, cols] = jnp.dot(hb[0:k * sub, :], wd, preferred_element_type=F32)
                    if k * sub < rows:
                        out_ref[k * sub:, cols] = jnp.zeros((rows - k * sub, kc), F32)
            refill(dn_copy, c, n_dn, nc)


def _experts(h_p, h_s, g_ffn, w_gate, w_up, w_down, unit_expert, unit_rows, buf_tok, nused, n_units):
    t, d = h_p.shape
    de = w_gate.shape[2]
    rows = MOE_ROWS
    assert t >= rows
    nc = MOE_CHUNKS
    kc = d // nc
    n_in, n_dn = MOE_RING_IN, MOE_RING_DN
    assert (2 * nc) % n_in == 0 and nc % n_dn == 0
    gs = pltpu.PrefetchScalarGridSpec(
        num_scalar_prefetch=4,
        grid=(n_units,),
        in_specs=[pl.BlockSpec(memory_space=pl.ANY),
                  pl.BlockSpec(memory_space=pl.ANY),
                  pl.BlockSpec((1, d), lambda u, ue, rw, tok, nu: (0, 0)),
                  pl.BlockSpec(memory_space=pl.ANY),
                  pl.BlockSpec(memory_space=pl.ANY),
                  pl.BlockSpec(memory_space=pl.ANY)],
        out_specs=pl.BlockSpec((rows, d), lambda u, ue, rw, tok, nu: (jnp.minimum(u, nu[0] - 1), 0)),
        scratch_shapes=[pltpu.VMEM((rows, d), F32), pltpu.VMEM((nc, rows, kc), BF16),
                        pltpu.VMEM((rows, de), F32), pltpu.VMEM((rows, de), F32),
                        pltpu.VMEM((rows, de), BF16),
                        pltpu.VMEM((n_in, kc, de), F32), pltpu.VMEM((n_dn, de, kc), F32),
                        pltpu.SemaphoreType.DMA((1,)), pltpu.SemaphoreType.DMA((n_in,)),
                        pltpu.SemaphoreType.DMA((n_dn,))],
    )
    vmem = (n_in + n_dn) * kc * de * 4 + 2 * rows * d * 4 + rows * d * 6 + rows * de * 10 + 3 * kc * de * 2 \
        + rows * d * 4
    return pl.pallas_call(
        _expert_kernel,
        out_shape=jax.ShapeDtypeStruct((n_units * rows, d), F32),
        grid_spec=gs,
        compiler_params=_params(("arbitrary",), vmem),
        name="moe_experts",
    )(unit_expert, unit_rows, buf_tok, nused, h_p, h_s, g_ffn, w_gate, w_up, w_down)


def _combine_kernel(slot_ref, h_ref, meta_ref, gfin_ref, eo_hbm, y_ref, obuf, sem, *, tile0):
    i = pl.program_id(0)
    n = pl.num_programs(0)
    rows = h_ref.shape[0]

    def gather(tile, buf):
        base = (tile0 + tile) * rows * 2

        def body(r8, carry):
            for j in range(8):
                r = r8 * 8 + j
                for k in range(2):
                    s = slot_ref[base + 2 * r + k]
                    pltpu.make_async_copy(eo_hbm.at[pl.ds(s, 1), :], obuf.at[buf, k, pl.ds(r, 1), :],
                                          sem.at[buf]).start()
            return carry
        lax.fori_loop(0, rows // 8, body, 0)

    buf = i % 2

    @pl.when(i == 0)
    def _():
        gather(0, 0)

    @pl.when(i + 1 < n)
    def _():
        gather(i + 1, 1 - buf)

    for k in range(2):
        pltpu.make_async_copy(eo_hbm.at[pl.ds(0, rows), :], obuf.at[buf, k], sem.at[buf]).wait()
    meta = meta_ref[...]
    y = meta[:, 2:3] * obuf[buf, 0] + meta[:, 3:4] * obuf[buf, 1]
    hh = h_ref[...] + y
    hn = hh * lax.rsqrt(jnp.mean(hh * hh, axis=-1, keepdims=True) + EPS)
    y_ref[...] = hn * gfin_ref[...]


def _combine(h, meta, g_final, expert_out, slots, tok0):
    n_tok, d = h.shape
    rows = _pick(n_tok, COMBINE_ROWS)
    assert tok0 % rows == 0 and n_tok % rows == 0
    tile0 = tok0 // rows
    gs = pltpu.PrefetchScalarGridSpec(
        num_scalar_prefetch=1,
        grid=(n_tok // rows,),
        in_specs=[pl.BlockSpec((rows, d), lambda i, sl: (i, 0)),
                  pl.BlockSpec((rows, LANES), lambda i, sl: (i, 0)),
                  pl.BlockSpec((1, d), lambda i, sl: (0, 0)),
                  pl.BlockSpec(memory_space=pl.ANY)],
        out_specs=pl.BlockSpec((rows, d), lambda i, sl: (i, 0)),
        scratch_shapes=[pltpu.VMEM((2, 2, rows, d), F32), pltpu.SemaphoreType.DMA((2,))],
    )
    vmem = 4 * rows * d * 4 + 4 * rows * d * 4 + 4 * rows * d * 4
    return pl.pallas_call(
        functools.partial(_combine_kernel, tile0=tile0),
        out_shape=jax.ShapeDtypeStruct((n_tok, d), F32),
        grid_spec=gs,
        compiler_params=_params(("arbitrary",), vmem),
        name="moe_combine",
    )(slots, h, meta, g_final, expert_out)


def _dispatch_plan(expert_ids, n_units):
    t = expert_ids.shape[0]
    e_flat = expert_ids.astype(jnp.int32).reshape(-1)
    onehot = (e_flat[:, None] == jnp.arange(N_EXPERTS, dtype=jnp.int32)[None, :]).astype(jnp.int32)
    csum = jnp.cumsum(onehot, axis=0)
    rank = jnp.take_along_axis(csum, e_flat[:, None], axis=1)[:, 0] - 1
    counts = csum[-1]
    nunit = (counts + MOE_ROWS - 1) // MOE_ROWS
    unit_end = jnp.cumsum(nunit)
    unit_start = unit_end - nunit
    slots = unit_start[e_flat] * MOE_ROWS + rank
    nused = unit_end[-1]
    tok = jnp.arange(2 * t, dtype=jnp.int32) // 2
    buf_tok = jnp.zeros((n_units * MOE_ROWS,), jnp.int32).at[slots].set(tok)
    unit_ids = jnp.minimum(jnp.arange(n_units, dtype=jnp.int32), nused - 1)
    unit_expert = jnp.minimum(jnp.searchsorted(unit_end, unit_ids, side="right"), N_EXPERTS - 1)
    unit_rows = jnp.clip(counts[unit_expert] - (unit_ids - unit_start[unit_expert]) * MOE_ROWS, 0, MOE_ROWS)
    return (unit_expert.astype(jnp.int32), unit_rows.astype(jnp.int32), buf_tok,
            nused.reshape(1).astype(jnp.int32), slots.astype(jnp.int32))


def kernel(x_prompt, x_sample, cache_k, cache_v, state_conv, cache_mem_k, cache_mem_v, page_table, mem_prompt, g_mix, w_in, lambda_q1, lambda_k1, lambda_q2, lambda_k2, g_subln, w_conv, w_out, g_mem, w_ck, w_cv, g_cross, w_cq, w_co, g_ffn, w_router_group, b_router_group, w_router_expert, b_router_expert, w_e_gate, w_e_up, w_e_down, g_final):
    depth = g_mix.shape[0]
    assert depth == 1, "single-layer trunk only"
    batch, seq, d = x_prompt.shape
    bd, dec_seq, _ = x_sample.shape
    assert dec_seq == 1
    n_pool, page = cache_k.shape[1], cache_k.shape[2]
    n_heads = cache_k.shape[3]
    aw = n_heads * HEAD_DIM
    cw = w_conv.shape[2]
    cross_w = w_cq.shape[2]
    n_mem = mem_prompt.shape[1]
    past_len = page_table.shape[1] * page
    lam_init = 0.8 - 0.6 * math.exp(-0.3 * 0)
    tp, ts = batch * seq, bd

    w_in_b = w_in.reshape(w_in.shape[1:]).astype(BF16)
    w_out_b = w_out.reshape(w_out.shape[1:]).astype(BF16)
    w_ck_b, w_cv_b, w_cq_b, w_co_b = (w.reshape(w.shape[1:]).astype(BF16) for w in (w_ck, w_cv, w_cq, w_co))
    lams = tuple(v.reshape(1, QK_DIM) for v in (lambda_q1[0], lambda_k1[0], lambda_q2[0], lambda_k2[0]))
    gsub = g_subln[0].reshape(1, HEAD_DIM)
    attn_scale = QK_DIM ** -0.5
    cross_scale = (cross_w // N_CROSS_HEADS) ** -0.5
    col_q, col_k, col_v = 0, aw, 2 * aw
    col_b, col_c, col_x = 3 * aw, 3 * aw + cw, 3 * aw + 2 * cw

    def in_proj(a, rope, period):
        q = _matmul([a], w_in_b, col_q, aw, out_dtype=BF16, mode="rope", scale=attn_scale, rope=rope,
                    rope_period=period, name="in_proj_q")
        k = _matmul([a], w_in_b, col_k, aw, out_dtype=F32, mode="rope", rope=rope, rope_period=period,
                    name="in_proj_k")
        v = _matmul([a], w_in_b, col_v, aw, out_dtype=F32, name="in_proj_v")
        return q, k, v

    def after_mixer(x2d, o_attn, g_conv, mem_k, mem_v, cross_fn):
        h1 = _matmul([o_attn, g_conv], w_out_b, 0, d, out_dtype=F32, mode="residual", residual=x2d,
                     bn_pref=512, name="out_proj")
        a2 = _rmsnorm_bf16(h1, g_cross[0])
        qc = _matmul([a2], w_cq_b, 0, cross_w, out_dtype=BF16, scale=cross_scale, name="cross_q")
        oc = cross_fn(qc, mem_k, mem_v)
        return _matmul([oc], w_co_b, 0, d, out_dtype=F32, mode="residual", residual=h1, bn_pref=512,
                       name="cross_out")

    xp = x_prompt.reshape(tp, d)
    a_p = _rmsnorm_bf16(xp, g_mix[0])
    rope_p = _rope_tables(jnp.arange(seq, dtype=jnp.int32))
    q_p, k_p, v_p = in_proj(a_p, rope_p, seq)
    o_p = _causal_diff_attn(q_p.reshape(batch, seq, aw), k_p.reshape(batch, seq, aw),
                            v_p.reshape(batch, seq, aw), lams, gsub, lam_init).reshape(tp, aw)
    g_p, conv_p = _conv_branch_seq(a_p, w_in_b, col_b, col_c, col_x, w_conv[0], batch, seq)
    m_p = _rmsnorm_bf16(mem_prompt.reshape(batch * n_mem, d), g_mem[0])
    mem_k_p = _matmul([m_p], w_ck_b, 0, cross_w, out_dtype=F32, name="mem_k")
    mem_v_p = _matmul([m_p], w_cv_b, 0, cross_w, out_dtype=F32, name="mem_v")
    h2_p = after_mixer(
        xp, o_p, g_p, mem_k_p.reshape(batch, n_mem, cross_w), mem_v_p.reshape(batch, n_mem, cross_w),
        lambda qc, mk, mv: _cross_attn_seq(qc.reshape(batch, seq, cross_w), mk, mv).reshape(tp, cross_w))

    xs = x_sample.reshape(ts, d)
    a_s = _rmsnorm_bf16(xs, g_mix[0])
    rope_s = _rope_tables(jnp.full((ts,), past_len, dtype=jnp.int32))
    q_s, k_s, v_s = in_proj(a_s, rope_s, ts)
    hd3 = (ts, n_heads, HEAD_DIM)
    o_s = _paged_diff_attn(q_s.reshape(hd3), k_s.reshape(hd3), v_s.reshape(hd3),
                           cache_k.reshape(n_pool, page, n_heads, HEAD_DIM),
                           cache_v.reshape(n_pool, page, n_heads, HEAD_DIM),
                           page_table, lams, gsub, lam_init).reshape(ts, aw)
    st = state_conv[0]
    g_s, z_s = _conv_branch_step(a_s, w_in_b, col_b, col_c, col_x, w_conv[0], st[:, 0, :], st[:, 1, :])
    conv_s = jnp.stack([st[:, 1, :], z_s], axis=1)
    h2_s = after_mixer(xs, o_s, g_s, cache_mem_k.reshape(cache_mem_k.shape[1:]),
                       cache_mem_v.reshape(cache_mem_v.shape[1:]), _cross_attn_step)

    t_all = tp + ts
    w_r = jnp.concatenate([w_router_group[0], w_router_expert[0],
                           jnp.zeros((d, LANES - N_GROUPS - N_EXPERTS), F32)], axis=1)
    b_r = jnp.concatenate([b_router_group[0], b_router_expert[0],
                           jnp.zeros((LANES - N_GROUPS - N_EXPERTS,), F32)]).reshape(1, LANES)
    g_ffn2 = g_ffn[0].reshape(1, d)
    meta_p = _router(h2_p, g_ffn2, w_r, b_r)
    meta_s = _router(h2_s, g_ffn2, w_r, b_r)
    n_units = (2 * t_all + N_EXPERTS * (MOE_ROWS - 1) + MOE_ROWS - 1) // MOE_ROWS
    unit_expert, unit_rows, buf_tok, nused, slots = _dispatch_plan(
        jnp.concatenate([meta_p[:, 0:2], meta_s[:, 0:2]], axis=0), n_units)
    expert_out = _experts(h2_p, h2_s, g_ffn2, w_e_gate.reshape(w_e_gate.shape[1:]),
                          w_e_up.reshape(w_e_up.shape[1:]), w_e_down.reshape(w_e_down.shape[1:]),
                          unit_expert, unit_rows, buf_tok, nused, n_units)
    g_fin = g_final.reshape(1, d)
    y_p = _combine(h2_p, meta_p, g_fin, expert_out, slots, 0)
    y_s = _combine(h2_s, meta_s, g_fin, expert_out, slots, tp)

    return (y_p.reshape(batch, seq, d), y_s.reshape(bd, 1, d),
            k_p.reshape(1, batch, seq, n_heads, HEAD_DIM), v_p.reshape(1, batch, seq, n_heads, HEAD_DIM),
            conv_p.reshape(1, batch, CONV_K - 1, cw),
            mem_k_p.reshape(1, batch, n_mem, N_CROSS_HEADS, cross_w // N_CROSS_HEADS),
            mem_v_p.reshape(1, batch, n_mem, N_CROSS_HEADS, cross_w // N_CROSS_HEADS),
            k_s.reshape(1, bd, 1, n_heads, HEAD_DIM), v_s.reshape(1, bd, 1, n_heads, HEAD_DIM),
            conv_s.reshape(1, bd, CONV_K - 1, cw))
```

```python
import functools
import math

import jax
import jax.numpy as jnp
from jax import lax
from jax.experimental import pallas as pl
from jax.experimental.pallas import tpu as pltpu

F32 = jnp.float32
BF16 = jnp.bfloat16

EPS = 1e-6
HEAD_DIM = 128
QK_DIM = HEAD_DIM // 2
ROPE_DIM = QK_DIM // 4
ROPE_HALF = ROPE_DIM // 2
ROPE_THETA = 500000.0
N_CROSS_HEADS = 4
N_GROUPS = 8
EXPERTS_PER_GROUP = 8
N_EXPERTS = N_GROUPS * EXPERTS_PER_GROUP
CONV_K = 3

LANES = 128
V7X_VMEM_BYTES = 64 * 1024 * 1024
VMEM_CAP = V7X_VMEM_BYTES - 6 * 1024 * 1024
MOE_ROWS = 384
MOE_SUB = 128
MOE_CHUNKS = 4
MOE_RING_IN = 8
MOE_RING_DN = 4
COMBINE_ROWS = 128
PAGES_PER_STEP = 8


def _pick(n, pref):
    if n <= pref:
        return n
    for d in range(pref, 7, -1):
        if n % d == 0 and d % 8 == 0:
            return d
    return n


def _params(sem, vmem_est):
    limit = int(min(VMEM_CAP, max(32 * 1024 * 1024, vmem_est * 5 // 4 + (2 << 20))))
    return pltpu.CompilerParams(dimension_semantics=sem, vmem_limit_bytes=limit)


def _nt_dot(a, b):
    return lax.dot_general(a, b, (((1,), (1,)), ((), ())), preferred_element_type=F32)


def _rmsnorm_kernel(x_ref, g_ref, o_ref):
    x = x_ref[...]
    y = x * lax.rsqrt(jnp.mean(x * x, axis=-1, keepdims=True) + EPS)
    o_ref[...] = (y * g_ref[...]).astype(o_ref.dtype)


def _rmsnorm_bf16(x, g):
    m, d = x.shape
    bm = _pick(m, 256)
    return pl.pallas_call(
        _rmsnorm_kernel,
        out_shape=jax.ShapeDtypeStruct((m, d), BF16),
        grid=(m // bm,),
        in_specs=[pl.BlockSpec((bm, d), lambda i: (i, 0)),
                  pl.BlockSpec((1, d), lambda i: (0, 0))],
        out_specs=pl.BlockSpec((bm, d), lambda i: (i, 0)),
        compiler_params=_params(("parallel",), 2 * bm * d * 6),
        name="rmsnorm_bf16",
    )(x, g.reshape(1, d))


def _mm_kernel(*refs, k_splits, mode, scale):
    n_lhs = len(k_splits)
    lhs = refs[:n_lhs]
    w_ref = refs[n_lhs]
    extra = refs[n_lhs + 1:-1]
    o_ref = refs[-1]
    acc = None
    k0 = 0
    for a_ref, kk in zip(lhs, k_splits):
        part = jnp.dot(a_ref[...], w_ref[k0:k0 + kk, :], preferred_element_type=F32)
        acc = part if acc is None else acc + part
        k0 += kk
    if mode == "rope":
        cos_ref, sa_ref, sb_ref = extra
        cos, sa, sb = cos_ref[...], sa_ref[...], sb_ref[...]
        for c in range(acc.shape[1] // LANES):
            blk = acc[:, c * LANES:(c + 1) * LANES]
            up = pltpu.roll(blk, LANES - ROPE_HALF, axis=1)
            dn = pltpu.roll(blk, ROPE_HALF, axis=1)
            rot = blk * cos + up * sa + dn * sb
            if scale != 1.0:
                rot = rot * scale
            o_ref[:, c * LANES:(c + 1) * LANES] = rot.astype(o_ref.dtype)
        return
    if mode == "residual":
        acc = extra[0][...] + acc
    if scale != 1.0:
        acc = acc * scale
    o_ref[...] = acc.astype(o_ref.dtype)


def _matmul(lhs_list, w, col0, n, *, out_dtype, mode="plain", scale=1.0, rope=None,
            rope_period=None, residual=None, bm_pref=1024, bn_pref=1024, name="matmul"):
    m = lhs_list[0].shape[0]
    k_splits = tuple(a.shape[1] for a in lhs_list)
    k_total = sum(k_splits)
    assert w.shape[0] == k_total
    bm = _pick(m, bm_pref)
    bn = _pick(n, bn_pref)
    assert col0 % bn == 0 and n % bn == 0 and m % bm == 0
    joff = col0 // bn
    in_specs = [pl.BlockSpec((bm, kk), lambda i, j: (i, 0)) for kk in k_splits]
    in_specs.append(pl.BlockSpec((k_total, bn), lambda i, j: (0, j + joff)))
    args = list(lhs_list) + [w]
    vmem = 2 * (bm * k_total * 2 + k_total * bn * 2 + bm * bn * 4) + 2 * bm * bn * 4
    if mode == "rope":
        assert rope_period % bm == 0 or bm % rope_period == 0
        if bm > rope_period:
            rope = tuple(jnp.tile(t, (bm // rope_period, 1)) for t in rope)
            nper = 1
        else:
            nper = rope_period // bm
        in_specs += [pl.BlockSpec((bm, LANES), lambda i, j: (i % nper, 0))] * 3
        args += list(rope)
    elif mode == "residual":
        in_specs.append(pl.BlockSpec((bm, bn), lambda i, j: (i, j)))
        args.append(residual)
        vmem += 2 * bm * bn * 4
    return pl.pallas_call(
        functools.partial(_mm_kernel, k_splits=k_splits, mode=mode, scale=scale),
        out_shape=jax.ShapeDtypeStruct((m, n), out_dtype),
        grid=(m // bm, n // bn),
        in_specs=in_specs,
        out_specs=pl.BlockSpec((bm, bn), lambda i, j: (i, j)),
        compiler_params=_params(("parallel", "parallel"), vmem),
        name=name,
    )(*args)


def _rope_tables(pos):
    inv_freq = ROPE_THETA ** (-jnp.arange(ROPE_HALF, dtype=F32) * (2.0 / ROPE_DIM))
    ang = pos.astype(F32)[:, None] * inv_freq[None, :]
    cos, sin = jnp.cos(ang), jnp.sin(ang)
    n = pos.shape[0]
    one = jnp.ones((n, QK_DIM - ROPE_DIM), F32)
    zero = jnp.zeros((n, QK_DIM - ROPE_DIM), F32)
    z8 = jnp.zeros((n, ROPE_HALF), F32)
    cos64 = jnp.concatenate([cos, cos, one], axis=1)
    up64 = jnp.concatenate([-sin, z8, zero], axis=1)
    dn64 = jnp.concatenate([z8, sin, zero], axis=1)
    rep = LANES // QK_DIM
    return tuple(jnp.tile(t, (1, rep)) for t in (cos64, up64, dn64))


def _conv_seq_kernel(a_ref, wb_ref, wc_ref, wx_ref, wconv_ref, g_ref, nc_ref, zbuf, *, tiles_per_seq):
    i = pl.program_id(1)
    bm = a_ref.shape[0]
    a = a_ref[...]
    cb = jnp.dot(a, wb_ref[...], preferred_element_type=F32)
    cc = jnp.dot(a, wc_ref[...], preferred_element_type=F32)
    cx = jnp.dot(a, wx_ref[...], preferred_element_type=F32)
    z = cc * cx
    first = (i % tiles_per_seq) == 0

    @pl.when(first)
    def _():
        zbuf[0:8, :] = jnp.zeros((8, zbuf.shape[1]), F32)

    @pl.when(jnp.logical_not(first))
    def _():
        zbuf[0:8, :] = zbuf[bm:bm + 8, :]

    zbuf[8:8 + bm, :] = z
    w = wconv_ref[...]
    y = w[0:1, :] * zbuf[6:6 + bm, :] + w[1:2, :] * zbuf[7:7 + bm, :]
    y = y + w[2:3, :] * z
    g_ref[...] = (cb * y).astype(g_ref.dtype)

    @pl.when((i % tiles_per_seq) == tiles_per_seq - 1)
    def _():
        nc_ref[0] = z[bm - (CONV_K - 1):, :]


def _conv_branch_seq(a, w, col_b, col_c, col_x, w_conv, batch, seq):
    m, k = a.shape
    cw = w_conv.shape[1]
    assert seq >= 8
    bm = _pick(seq, 512)
    bc = _pick(cw, 512)
    tps = seq // bm
    ob, oc, ox = col_b // bc, col_c // bc, col_x // bc
    vmem = 2 * (bm * k * 2 + 3 * k * bc * 2 + bm * bc * 2) + 6 * bm * bc * 4
    return pl.pallas_call(
        functools.partial(_conv_seq_kernel, tiles_per_seq=tps),
        out_shape=(jax.ShapeDtypeStruct((m, cw), BF16),
                   jax.ShapeDtypeStruct((batch, CONV_K - 1, cw), F32)),
        grid=(cw // bc, m // bm),
        in_specs=[pl.BlockSpec((bm, k), lambda c, i: (i, 0)),
                  pl.BlockSpec((k, bc), lambda c, i: (0, ob + c)),
                  pl.BlockSpec((k, bc), lambda c, i: (0, oc + c)),
                  pl.BlockSpec((k, bc), lambda c, i: (0, ox + c)),
                  pl.BlockSpec((CONV_K, bc), lambda c, i: (0, c))],
        out_specs=(pl.BlockSpec((bm, bc), lambda c, i: (i, c)),
                   pl.BlockSpec((1, CONV_K - 1, bc), lambda c, i: (i // tps, 0, c))),
        scratch_shapes=[pltpu.VMEM((bm + 8, bc), F32)],
        compiler_params=_params(("parallel", "arbitrary"), vmem),
        name="conv_branch_seq",
    )(a, w, w, w, w_conv)


def _conv_step_kernel(a_ref, wb_ref, wc_ref, wx_ref, wconv_ref, s0_ref, s1_ref, g_ref, z_ref):
    a = a_ref[...]
    cb = jnp.dot(a, wb_ref[...], preferred_element_type=F32)
    cc = jnp.dot(a, wc_ref[...], preferred_element_type=F32)
    cx = jnp.dot(a, wx_ref[...], preferred_element_type=F32)
    z = cc * cx
    w = wconv_ref[...]
    y = w[0:1, :] * s0_ref[...] + w[1:2, :] * s1_ref[...]
    y = y + w[2:3, :] * z
    g_ref[...] = (cb * y).astype(g_ref.dtype)
    z_ref[...] = z


def _conv_branch_step(a, w, col_b, col_c, col_x, w_conv, s0, s1):
    m, k = a.shape
    cw = w_conv.shape[1]
    bc = _pick(cw, 512)
    ob, oc, ox = col_b // bc, col_c // bc, col_x // bc
    vmem = 2 * (m * k * 2 + 3 * k * bc * 2 + 5 * m * bc * 4)
    return pl.pallas_call(
        _conv_step_kernel,
        out_shape=(jax.ShapeDtypeStruct((m, cw), BF16), jax.ShapeDtypeStruct((m, cw), F32)),
        grid=(cw // bc,),
        in_specs=[pl.BlockSpec((m, k), lambda c: (0, 0)),
                  pl.BlockSpec((k, bc), lambda c: (0, ob + c)),
                  pl.BlockSpec((k, bc), lambda c: (0, oc + c)),
                  pl.BlockSpec((k, bc), lambda c: (0, ox + c)),
                  pl.BlockSpec((CONV_K, bc), lambda c: (0, c)),
                  pl.BlockSpec((m, bc), lambda c: (0, c)),
                  pl.BlockSpec((m, bc), lambda c: (0, c))],
        out_specs=(pl.BlockSpec((m, bc), lambda c: (0, c)),
                   pl.BlockSpec((m, bc), lambda c: (0, c))),
        compiler_params=_params(("parallel",), vmem),
        name="conv_branch_step",
    )(a, w, w, w, w_conv, s0, s1)


def _diff_lambda(lq1_ref, lk1_ref, lq2_ref, lk2_ref, lam_init):
    a = jnp.sum(lq1_ref[...] * lk1_ref[...], axis=-1, keepdims=True)
    b = jnp.sum(lq2_ref[...] * lk2_ref[...], axis=-1, keepdims=True)
    return jnp.exp(a) - jnp.exp(b) + lam_init


def _subln(d, gsub, lam_init):
    dn = d * lax.rsqrt(jnp.mean(d * d, axis=-1, keepdims=True) + EPS)
    return dn * gsub * (1.0 - lam_init)


def _causal_attn_kernel(q_ref, k_ref, v_ref, lq1, lk1, lq2, lk2, gsub_ref, o_ref, kb, vb, *, bq, lam_init):
    seq = q_ref.shape[1]
    kb[...] = k_ref[0].astype(BF16)
    vb[:, 0:HEAD_DIM] = v_ref[0].astype(BF16)
    vb[:, HEAD_DIM:] = jnp.ones((seq, HEAD_DIM), BF16)
    lam = _diff_lambda(lq1, lk1, lq2, lk2, lam_init)
    gsub = gsub_ref[...]
    lane = lax.broadcasted_iota(jnp.int32, (bq, HEAD_DIM), 1)
    row = lax.broadcasted_iota(jnp.int32, (2 * bq, bq), 0)
    col = lax.broadcasted_iota(jnp.int32, (2 * bq, bq), 1)
    tri = col <= jnp.where(row >= bq, row - bq, row)

    def scores(qi):
        q = q_ref[0, qi * bq:(qi + 1) * bq, :]
        zero = jnp.zeros_like(q)
        qs = jnp.concatenate([jnp.where(lane < QK_DIM, q, zero), jnp.where(lane >= QK_DIM, q, zero)], axis=0)
        past = qi * bq
        s_diag = jnp.where(tri, _nt_dot(qs, kb[past:past + bq, :]), -jnp.inf)
        s_past = _nt_dot(qs, kb[0:past, :]) if past else None
        return s_diag, s_past

    nq = seq // bq
    nxt = scores(0)
    for qi in range(nq):
        s_diag, s_past = nxt
        if qi + 1 < nq:
            nxt = scores(qi + 1)
        past = qi * bq
        m = jnp.max(s_diag, axis=-1, keepdims=True)
        if past:
            m = jnp.maximum(m, jnp.max(s_past, axis=-1, keepdims=True))
        p_diag = jnp.exp((s_diag - m).astype(BF16))
        ol = jnp.dot(p_diag, vb[past:past + bq, :], preferred_element_type=F32)
        if past:
            p_past = jnp.exp((s_past - m).astype(BF16))
            ol = ol + jnp.dot(p_past, vb[0:past, :], preferred_element_type=F32)
        o = ol[:, 0:HEAD_DIM] / ol[:, HEAD_DIM:HEAD_DIM + 1]
        d = o[:bq] - lam * o[bq:]
        o_ref[0, qi * bq:(qi + 1) * bq, :] = _subln(d, gsub, lam_init).astype(o_ref.dtype)


def _causal_diff_attn(q, k, v, lams, gsub, lam_init):
    b, s, w = q.shape
    h = w // HEAD_DIM
    bq = _pick(s, 256)
    blk = lambda: pl.BlockSpec((1, s, HEAD_DIM), lambda bi, hi: (bi, 0, hi))
    small = lambda n: pl.BlockSpec((1, n), lambda bi, hi: (0, 0))
    vmem = 2 * s * HEAD_DIM * (2 + 4 + 4 + 2) + 2 * s * HEAD_DIM * 2 + 6 * 2 * bq * s * 4
    return pl.pallas_call(
        functools.partial(_causal_attn_kernel, bq=bq, lam_init=lam_init),
        out_shape=jax.ShapeDtypeStruct((b, s, w), BF16),
        grid=(b, h),
        in_specs=[blk(), blk(), blk(), small(QK_DIM), small(QK_DIM), small(QK_DIM), small(QK_DIM),
                  small(HEAD_DIM)],
        out_specs=blk(),
        scratch_shapes=[pltpu.VMEM((s, HEAD_DIM), BF16), pltpu.VMEM((s, 2 * HEAD_DIM), BF16)],
        compiler_params=_params(("parallel", "parallel"), vmem),
        name="causal_diff_attn",
    )(q, k, v, *lams, gsub)


def _paged_attn_kernel(pt_ref, q_ref, kc_ref, vc_ref, bias_ref, *rest, n_heads, ppb, lam_init):
    kp_refs, vp_refs = rest[:ppb], rest[ppb:2 * ppb]
    lq1, lk1, lq2, lk2, gsub_ref, o_ref, qm, m_s, l_s, acc = rest[2 * ppb:]
    p = pl.program_id(1)
    page = kp_refs[0].shape[1]

    @pl.when(p == 0)
    def _():
        q = q_ref[0].astype(F32)
        lane = lax.broadcasted_iota(jnp.int32, q.shape, 1)
        qm[0:n_heads, :] = jnp.where(lane < QK_DIM, q, 0.0).astype(BF16)
        qm[n_heads:, :] = jnp.where(lane >= QK_DIM, q, 0.0).astype(BF16)
        m_s[...] = jnp.full(m_s.shape, -jnp.inf, F32)
        l_s[...] = jnp.zeros(l_s.shape, F32)
        acc[...] = jnp.zeros(acc.shape, F32)

    qv = qm[...]
    bias = bias_ref[...]
    ss = [_nt_dot(qv, kp_refs[j][0].reshape(page * n_heads, HEAD_DIM).astype(BF16)) + bias
          for j in range(ppb)]
    m_old = m_s[...]
    m_new = m_old
    for s in ss:
        m_new = jnp.maximum(m_new, jnp.max(s, axis=-1, keepdims=True))
    corr = jnp.exp(m_old - m_new)
    l = l_s[...] * corr
    a = acc[...] * corr
    for j, s in enumerate(ss):
        pr = jnp.exp(s - m_new)
        l = l + jnp.sum(pr, axis=-1, keepdims=True)
        a = a + jnp.dot(pr.astype(BF16), vp_refs[j][0].reshape(page * n_heads, HEAD_DIM).astype(BF16),
                        preferred_element_type=F32)
    l_s[...] = l
    acc[...] = a
    m_s[...] = m_new

    @pl.when(p == pl.num_programs(1) - 1)
    def _():
        kc = kc_ref[0].astype(BF16).astype(F32)
        vc = vc_ref[0].astype(BF16).astype(F32)
        kc2 = jnp.concatenate([kc, kc], axis=0)
        vc2 = jnp.concatenate([vc, vc], axis=0)
        s_c = jnp.sum(qm[...].astype(F32) * kc2, axis=-1, keepdims=True)
        m_old = m_s[...]
        m_new = jnp.maximum(m_old, s_c)
        corr = jnp.exp(m_old - m_new)
        p_c = jnp.exp(s_c - m_new)
        l = l_s[...] * corr + p_c
        a = acc[...] * corr + p_c.astype(BF16).astype(F32) * vc2
        o = a / l
        lam = _diff_lambda(lq1, lk1, lq2, lk2, lam_init)
        d = o[:n_heads] - lam * o[n_heads:]
        o_ref[0] = _subln(d, gsub_ref[...], lam_init).astype(o_ref.dtype)


def _paged_diff_attn(q, k_cur, v_cur, cache_k, cache_v, page_table, lams, gsub, lam_init):
    bd, n_heads, _ = q.shape
    n_pages = page_table.shape[1]
    page = cache_k.shape[1]
    ppb = next(c for c in (PAGES_PER_STEP, 4, 2, 1) if n_pages % c == 0)
    nrow = 2 * n_heads
    ncol = page * n_heads
    own = ((jnp.arange(ncol, dtype=jnp.int32)[None, :] % n_heads)
           == (jnp.arange(nrow, dtype=jnp.int32)[:, None] % n_heads))
    bias = jnp.where(own, 0.0, -jnp.inf).astype(F32)
    row = lambda: pl.BlockSpec((1, n_heads, HEAD_DIM), lambda b, p, pt: (b, 0, 0))

    def pg(j):
        return pl.BlockSpec((1, page, n_heads, HEAD_DIM),
                            lambda b, p, pt: (pt[b * n_pages + p * ppb + j], 0, 0, 0))

    small = lambda n: pl.BlockSpec((1, n), lambda b, p, pt: (0, 0))
    gs = pltpu.PrefetchScalarGridSpec(
        num_scalar_prefetch=1,
        grid=(bd, n_pages // ppb),
        in_specs=[row(), row(), row(), pl.BlockSpec((nrow, ncol), lambda b, p, pt: (0, 0))]
        + [pg(j) for j in range(ppb)] + [pg(j) for j in range(ppb)]
        + [small(QK_DIM), small(QK_DIM), small(QK_DIM), small(QK_DIM), small(HEAD_DIM)],
        out_specs=row(),
        scratch_shapes=[pltpu.VMEM((nrow, HEAD_DIM), BF16), pltpu.VMEM((nrow, 1), F32),
                        pltpu.VMEM((nrow, 1), F32), pltpu.VMEM((nrow, HEAD_DIM), F32)],
    )
    vmem = 4 * ppb * ncol * HEAD_DIM * 4 + 2 * ppb * ncol * HEAD_DIM * 2 + (4 + 3 * ppb) * nrow * ncol * 4
    return pl.pallas_call(
        functools.partial(_paged_attn_kernel, n_heads=n_heads, ppb=ppb, lam_init=lam_init),
        out_shape=jax.ShapeDtypeStruct((bd, n_heads, HEAD_DIM), BF16),
        grid_spec=gs,
        compiler_params=_params(("parallel", "arbitrary"), vmem),
        name="paged_diff_attn",
    )(page_table.reshape(-1), q, k_cur, v_cur, bias, *([cache_k] * ppb), *([cache_v] * ppb), *lams, gsub)


def _cross_attn_kernel(q_ref, mk_ref, mv_ref, o_ref, mkb, mvb):
    @pl.when(pl.program_id(1) == 0)
    def _():
        mkb[...] = mk_ref[0].astype(BF16)
        mvb[...] = mv_ref[0].astype(BF16)

    hd = q_ref.shape[2] // N_CROSS_HEADS
    for h in range(N_CROSS_HEADS):
        sl = slice(h * hd, (h + 1) * hd)
        s = _nt_dot(q_ref[0, :, sl], mkb[:, sl])
        p = jnp.exp(s - jnp.max(s, axis=-1, keepdims=True))
        l = jnp.sum(p, axis=-1, keepdims=True)
        o = jnp.dot(p.astype(BF16), mvb[:, sl], preferred_element_type=F32) / l
        o_ref[0, :, sl] = o.astype(o_ref.dtype)


def _cross_attn_seq(q, mem_k, mem_v):
    b, s, cw = q.shape
    n_mem = mem_k.shape[1]
    bq = _pick(s, 512)
    vmem = 2 * (2 * bq * cw * 2 + 2 * n_mem * cw * 4) + 2 * n_mem * cw * 2 + 6 * bq * n_mem * 4
    return pl.pallas_call(
        _cross_attn_kernel,
        out_shape=jax.ShapeDtypeStruct((b, s, cw), BF16),
        grid=(b, s // bq),
        in_specs=[pl.BlockSpec((1, bq, cw), lambda bi, i: (bi, i, 0)),
                  pl.BlockSpec((1, n_mem, cw), lambda bi, i: (bi, 0, 0)),
                  pl.BlockSpec((1, n_mem, cw), lambda bi, i: (bi, 0, 0))],
        out_specs=pl.BlockSpec((1, bq, cw), lambda bi, i: (bi, i, 0)),
        scratch_shapes=[pltpu.VMEM((n_mem, cw), BF16), pltpu.VMEM((n_mem, cw), BF16)],
        compiler_params=_params(("parallel", "arbitrary"), vmem),
        name="cross_attn_seq",
    )(q, mem_k, mem_v)


def _cross_step_kernel(q_ref, mk_hbm, mv_hbm, o_ref, kbuf, vbuf, sem):
    i = pl.program_id(0)
    n = pl.num_programs(0)
    bb, _, cw = q_ref.shape
    nh, hd = kbuf.shape[1], kbuf.shape[4]

    def copies(step, slot):
        out = []
        for h in range(nh):
            src = pl.ds(step * bb, bb)
            out.append(pltpu.make_async_copy(mk_hbm.at[src, :, h, :], kbuf.at[slot, h], sem.at[slot, 0]))
            out.append(pltpu.make_async_copy(mv_hbm.at[src, :, h, :], vbuf.at[slot, h], sem.at[slot, 1]))
        return out

    slot = i % 2

    @pl.when(i == 0)
    def _():
        for c in copies(0, 0):
            c.start()

    @pl.when(i + 1 < n)
    def _():
        for c in copies(i + 1, 1 - slot):
            c.start()

    for c in copies(i, slot):
        c.wait()
    for j in range(bb):
        for h in range(nh):
            sl = slice(h * hd, (h + 1) * hd)
            qh = jnp.broadcast_to(q_ref[j, :, sl], (8, hd))
            s = _nt_dot(qh, kbuf[slot, h, j].astype(BF16))
            p = jnp.exp(s - jnp.max(s, axis=-1, keepdims=True))
            l = jnp.sum(p, axis=-1, keepdims=True)
            o = jnp.dot(p.astype(BF16), vbuf[slot, h, j].astype(BF16), preferred_element_type=F32) / l
            o_ref[j, :, sl] = o[0:1].astype(o_ref.dtype)


def _cross_attn_step(q, mem_k, mem_v):
    bd, cw = q.shape
    n_mem, nh, hd = mem_k.shape[1:]
    assert nh == N_CROSS_HEADS and nh * hd == cw
    bb = 4 if bd % 4 == 0 else 1
    buf = pltpu.VMEM((2, nh, bb, n_mem, hd), F32)
    vmem = 2 * 2 * nh * bb * n_mem * hd * 4 + 4 * n_mem * hd * 4
    out = pl.pallas_call(
        _cross_step_kernel,
        out_shape=jax.ShapeDtypeStruct((bd, 1, cw), BF16),
        grid=(bd // bb,),
        in_specs=[pl.BlockSpec((bb, 1, cw), lambda i: (i, 0, 0)),
                  pl.BlockSpec(memory_space=pl.ANY), pl.BlockSpec(memory_space=pl.ANY)],
        out_specs=pl.BlockSpec((bb, 1, cw), lambda i: (i, 0, 0)),
        scratch_shapes=[buf, buf, pltpu.SemaphoreType.DMA((2, 2))],
        compiler_params=_params(("arbitrary",), vmem),
        name="cross_attn_step",
    )(q.reshape(bd, 1, cw), mem_k, mem_v)
    return out.reshape(bd, cw)


def _router_kernel(h_ref, g_ref, whi_ref, wlo_ref, br_ref, meta_ref):
    x = h_ref[...]
    xn = x * lax.rsqrt(jnp.mean(x * x, axis=-1, keepdims=True) + EPS) * g_ref[...]
    hi = xn.astype(BF16)
    lo = (xn - hi.astype(F32)).astype(BF16)
    whi = whi_ref[...]
    logits = (jnp.dot(hi, whi, preferred_element_type=F32)
              + (jnp.dot(hi, wlo_ref[...], preferred_element_type=F32)
                 + jnp.dot(lo, whi, preferred_element_type=F32)))
    logits = logits + br_ref[...]
    bm = logits.shape[0]
    lane = lax.broadcasted_iota(jnp.int32, (bm, LANES), 1)
    lanef = lane.astype(F32)
    neg = jnp.float32(-jnp.inf)
    big = jnp.float32(LANES)

    def top(mask):
        v = jnp.max(jnp.where(mask, logits, neg), axis=-1, keepdims=True)
        idx = jnp.min(jnp.where(jnp.logical_and(mask, logits == v), lanef, big), axis=-1, keepdims=True)
        return v, idx

    gmask = lane < N_GROUPS
    gmax, gidx = top(gmask)
    gsum = jnp.sum(jnp.where(gmask, jnp.exp(logits - gmax), 0.0), axis=-1, keepdims=True)
    g_top_p = 1.0 / gsum
    lo = N_GROUPS + gidx * EXPERTS_PER_GROUP
    emask = jnp.logical_and(lanef >= lo, lanef < lo + EXPERTS_PER_GROUP)
    v1, i1 = top(emask)
    v2, i2 = top(jnp.logical_and(emask, lanef != i1))
    r = jnp.exp(v2 - v1)
    den = 1.0 + r
    gate1 = g_top_p * (1.0 / den)
    gate2 = g_top_p * (r / den)
    meta = jnp.where(lane == 0, i1 - N_GROUPS,
                     jnp.where(lane == 1, i2 - N_GROUPS,
                               jnp.where(lane == 2, gate1, jnp.where(lane == 3, gate2, 0.0))))
    meta_ref[...] = meta


def _router(h, g, w_r, b_r):
    t, d = h.shape
    bm = _pick(t, 256)
    w_hi = w_r.astype(BF16)
    w_lo = (w_r - w_hi.astype(F32)).astype(BF16)
    vmem = 2 * (bm * d * 4 + 2 * d * LANES * 2) + 5 * bm * d * 4
    return pl.pallas_call(
        _router_kernel,
        out_shape=jax.ShapeDtypeStruct((t, LANES), F32),
        grid=(t // bm,),
        in_specs=[pl.BlockSpec((bm, d), lambda i: (i, 0)),
                  pl.BlockSpec((1, d), lambda i: (0, 0)),
                  pl.BlockSpec((d, LANES), lambda i: (0, 0)),
                  pl.BlockSpec((d, LANES), lambda i: (0, 0)),
                  pl.BlockSpec((1, LANES), lambda i: (0, 0))],
        out_specs=pl.BlockSpec((bm, LANES), lambda i: (i, 0)),
        compiler_params=_params(("parallel",), vmem),
        name="moe_router",
    )(h, g, w_hi, w_lo, b_r)


def _expert_kernel(ue_ref, rows_ref, tok_ref, nused_ref, h_hbm, hs_hbm, g_ref, wg_hbm, wu_hbm, wd_hbm, out_ref,
                   xraw, xb, hg, hu, hb, ring_in, ring_dn, sem_x, sem_in, sem_dn):
    u = pl.program_id(0)
    nc = xb.shape[0]
    kc = xb.shape[2]
    nused = nused_ref[0]
    rows = xraw.shape[0]
    sub = MOE_SUB
    tp = h_hbm.shape[0]
    n_in, n_dn = ring_in.shape[0], ring_dn.shape[0]

    def padded(n):
        return pl.multiple_of(((n + 7) // 8) * 8, 8)

    def gather(unit):
        def body(r8, carry):
            for j in range(8):
                r = r8 * 8 + j
                t = tok_ref[unit * rows + r]
                dst = xraw.at[pl.ds(r, 1), :]

                @pl.when(t < tp)
                def _():
                    pltpu.make_async_copy(h_hbm.at[pl.ds(t, 1), :], dst, sem_x.at[0]).start()

                @pl.when(t >= tp)
                def _():
                    pltpu.make_async_copy(hs_hbm.at[pl.ds(t - tp, 1), :], dst, sem_x.at[0]).start()
            return carry
        lax.fori_loop(0, padded(rows_ref[unit]) // 8, body, 0)

    def in_copy(unit, q):
        src = wg_hbm if q % 2 == 0 else wu_hbm
        slot = q % n_in
        return pltpu.make_async_copy(src.at[ue_ref[unit], pl.ds((q // 2) * kc, kc), :], ring_in.at[slot],
                                     sem_in.at[slot])

    def dn_copy(unit, q):
        slot = q % n_dn
        return pltpu.make_async_copy(wd_hbm.at[ue_ref[unit], :, pl.ds(q * kc, kc)], ring_dn.at[slot],
                                     sem_dn.at[slot])

    def refill(copy_fn, q, n_slots, n_chunks):
        nxt = q + n_slots
        if nxt < n_chunks:
            copy_fn(u, nxt).start()
        else:
            @pl.when(u + 1 < nused)
            def _():
                copy_fn(u + 1, nxt - n_chunks).start()

    @pl.when(u < nused)
    def _():
        nrows = rows_ref[u]
        nslab = (nrows + sub - 1) // sub

        @pl.when(u == 0)
        def _():
            xraw[...] = jnp.zeros(xraw.shape, F32)
            gather(0)
            for q in range(n_in):
                in_copy(0, q).start()
            for q in range(n_dn):
                dn_copy(0, q).start()

        n8 = padded(nrows)
        pltpu.make_async_copy(h_hbm.at[pl.ds(0, n8), :], xraw.at[pl.ds(0, n8), :], sem_x.at[0]).wait()
        for s in range(rows // sub):
            sl = slice(s * sub, (s + 1) * sub)

            @pl.when(nrows > s * sub)
            def _():
                x = xraw[sl, :]
                xn = (x * lax.rsqrt(jnp.mean(x * x, axis=-1, keepdims=True) + EPS) * g_ref[...]).astype(BF16)
                for j in range(nc):
                    xb[j, sl, :] = xn[:, j * kc:(j + 1) * kc]

        @pl.when(u + 1 < nused)
        def _():
            gather(u + 1)

        for c in range(nc):
            in_copy(u, 2 * c).wait()
            in_copy(u, 2 * c + 1).wait()
            wg = ring_in[(2 * c) % n_in].astype(BF16)
            wu = ring_in[(2 * c + 1) % n_in].astype(BF16)
            for k in range(1, rows // sub + 1):
                @pl.when(nslab == k)
                def _():
                    x = xb[c, 0:k * sub, :]
                    pg = jnp.dot(x, wg, preferred_element_type=F32)
                    pu = jnp.dot(x, wu, preferred_element_type=F32)
                    if c == 0:
                        hg[0:k * sub, :] = pg
                        hu[0:k * sub, :] = pu
                    else:
                        hg[0:k * sub, :] += pg
                        hu[0:k * sub, :] += pu
            refill(in_copy, 2 * c, n_in, 2 * nc)
            refill(in_copy, 2 * c + 1, n_in, 2 * nc)

        for k in range(1, rows // sub + 1):
            @pl.when(nslab == k)
            def _():
                g = hg[0:k * sub, :]
                hb[0:k * sub, :] = (g * (1.0 / (1.0 + jnp.exp(-g))) * hu[0:k * sub, :]).astype(BF16)

        for c in range(nc):
            dn_copy(u, c).wait()
            wd = ring_dn[c % n_dn].astype(BF16)
            cols = slice(c * kc, (c + 1) * kc)
            for k in range(1, rows // sub + 1):
                @pl.when(nslab == k)
                def _():
                    out_ref[0:k * sub, cols] = jnp.dot(hb[0:k * sub, :], wd, preferred_element_type=F32)
                    if k * sub < rows:
                        out_ref[k * sub:, cols] = jnp.zeros((rows - k * sub, kc), F32)
            refill(dn_copy, c, n_dn, nc)


def _experts(h_p, h_s, g_ffn, w_gate, w_up, w_down, unit_expert, unit_rows, buf_tok, nused, n_units):
    t, d = h_p.shape
    de = w_gate.shape[2]
    rows = MOE_ROWS
    assert t >= rows
    nc = MOE_CHUNKS
    kc = d // nc
    n_in, n_dn = MOE_RING_IN, MOE_RING_DN
    assert (2 * nc) % n_in == 0 and nc % n_dn == 0
    gs = pltpu.PrefetchScalarGridSpec(
        num_scalar_prefetch=4,
        grid=(n_units,),
        in_specs=[pl.BlockSpec(memory_space=pl.ANY),
                  pl.BlockSpec(memory_space=pl.ANY),
                  pl.BlockSpec((1, d), lambda u, ue, rw, tok, nu: (0, 0)),
                  pl.BlockSpec(memory_space=pl.ANY),
                  pl.BlockSpec(memory_space=pl.ANY),
                  pl.BlockSpec(memory_space=pl.ANY)],
        out_specs=pl.BlockSpec((rows, d), lambda u, ue, rw, tok, nu: (jnp.minimum(u, nu[0] - 1), 0)),
        scratch_shapes=[pltpu.VMEM((rows, d), F32), pltpu.VMEM((nc, rows, kc), BF16),
                        pltpu.VMEM((rows, de), F32), pltpu.VMEM((rows, de), F32),
                        pltpu.VMEM((rows, de), BF16),
                        pltpu.VMEM((n_in, kc, de), F32), pltpu.VMEM((n_dn, de, kc), F32),
                        pltpu.SemaphoreType.DMA((1,)), pltpu.SemaphoreType.DMA((n_in,)),
                        pltpu.SemaphoreType.DMA((n_dn,))],
    )
    vmem = (n_in + n_dn) * kc * de * 4 + 2 * rows * d * 4 + rows * d * 6 + rows * de * 10 + 3 * kc * de * 2 \
        + rows * d * 4
    return pl.pallas_call(
        _expert_kernel,
        out_shape=jax.ShapeDtypeStruct((n_units * rows, d), F32),
        grid_spec=gs,
        compiler_params=_params(("arbitrary",), vmem),
        name="moe_experts",
    )(unit_expert, unit_rows, buf_tok, nused, h_p, h_s, g_ffn, w_gate, w_up, w_down)


def _combine_kernel(slot_ref, h_ref, meta_ref, gfin_ref, eo_hbm, y_ref, obuf, sem, *, tile0):
    i = pl.program_id(0)
    n = pl.num_programs(0)
    rows = h_ref.shape[0]

    def gather(tile, buf):
        base = (tile0 + tile) * rows * 2

        def body(r8, carry):
            for j in range(8):
                r = r8 * 8 + j
                for k in range(2):
                    s = slot_ref[base + 2 * r + k]
                    pltpu.make_async_copy(eo_hbm.at[pl.ds(s, 1), :], obuf.at[buf, k, pl.ds(r, 1), :],
                                          sem.at[buf]).start()
            return carry
        lax.fori_loop(0, rows // 8, body, 0)

    buf = i % 2

    @pl.when(i == 0)
    def _():
        gather(0, 0)

    @pl.when(i + 1 < n)
    def _():
        gather(i + 1, 1 - buf)

    for k in range(2):
        pltpu.make_async_copy(eo_hbm.at[pl.ds(0, rows), :], obuf.at[buf, k], sem.at[buf]).wait()
    meta = meta_ref[...]
    y = meta[:, 2:3] * obuf[buf, 0] + meta[:, 3:4] * obuf[buf, 1]
    hh = h_ref[...] + y
    hn = hh * lax.rsqrt(jnp.mean(hh * hh, axis=-1, keepdims=True) + EPS)
    y_ref[...] = hn * gfin_ref[...]


def _combine(h, meta, g_final, expert_out, slots, tok0):
    n_tok, d = h.shape
    rows = _pick(n_tok, COMBINE_ROWS)
    assert tok0 % rows == 0 and n_tok % rows == 0
    tile0 = tok0 // rows
    gs = pltpu.PrefetchScalarGridSpec(
        num_scalar_prefetch=1,
        grid=(n_tok // rows,),
        in_specs=[pl.BlockSpec((rows, d), lambda i, sl: (i, 0)),
                  pl.BlockSpec((rows, LANES), lambda i, sl: (i, 0)),
                  pl.BlockSpec((1, d), lambda i, sl: (0, 0)),
                  pl.BlockSpec(memory_space=pl.ANY)],
        out_specs=pl.BlockSpec((rows, d), lambda i, sl: (i, 0)),
        scratch_shapes=[pltpu.VMEM((2, 2, rows, d), F32), pltpu.SemaphoreType.DMA((2,))],
    )
    vmem = 4 * rows * d * 4 + 4 * rows * d * 4 + 4 * rows * d * 4
    return pl.pallas_call(
        functools.partial(_combine_kernel, tile0=tile0),
        out_shape=jax.ShapeDtypeStruct((n_tok, d), F32),
        grid_spec=gs,
        compiler_params=_params(("arbitrary",), vmem),
        name="moe_combine",
    )(slots, h, meta, g_final, expert_out)


def _dispatch_plan(expert_ids, n_units):
    t = expert_ids.shape[0]
    e_flat = expert_ids.astype(jnp.int32).reshape(-1)
    onehot = (e_flat[:, None] == jnp.arange(N_EXPERTS, dtype=jnp.int32)[None, :]).astype(jnp.int32)
    csum = jnp.cumsum(onehot, axis=0)
    rank = jnp.take_along_axis(csum, e_flat[:, None], axis=1)[:, 0] - 1
    counts = csum[-1]
    nunit = (counts + MOE_ROWS - 1) // MOE_ROWS
    unit_end = jnp.cumsum(nunit)
    unit_start = unit_end - nunit
    slots = unit_start[e_flat] * MOE_ROWS + rank
    nused = unit_end[-1]
    tok = jnp.arange(2 * t, dtype=jnp.int32) // 2
    buf_tok = jnp.zeros((n_units * MOE_ROWS,), jnp.int32).at[slots].set(tok)
    unit_ids = jnp.minimum(jnp.arange(n_units, dtype=jnp.int32), nused - 1)
    unit_expert = jnp.minimum(jnp.searchsorted(unit_end, unit_ids, side="right"), N_EXPERTS - 1)
    unit_rows = jnp.clip(counts[unit_expert] - (unit_ids - unit_start[unit_expert]) * MOE_ROWS, 0, MOE_ROWS)
    return (unit_expert.astype(jnp.int32), unit_rows.astype(jnp.int32), buf_tok,
            nused.reshape(1).astype(jnp.int32), slots.astype(jnp.int32))


def kernel(x_prompt, x_sample, cache_k, cache_v, state_conv, cache_mem_k, cache_mem_v, page_table, mem_prompt, g_mix, w_in, lambda_q1, lambda_k1, lambda_q2, lambda_k2, g_subln, w_conv, w_out, g_mem, w_ck, w_cv, g_cross, w_cq, w_co, g_ffn, w_router_group, b_router_group, w_router_expert, b_router_expert, w_e_gate, w_e_up, w_e_down, g_final):
    depth = g_mix.shape[0]
    assert depth == 1, "single-layer trunk only"
    batch, seq, d = x_prompt.shape
    bd, dec_seq, _ = x_sample.shape
    assert dec_seq == 1
    n_pool, page = cache_k.shape[1], cache_k.shape[2]
    n_heads = cache_k.shape[3]
    aw = n_heads * HEAD_DIM
    cw = w_conv.shape[2]
    cross_w = w_cq.shape[2]
    n_mem = mem_prompt.shape[1]
    past_len = page_table.shape[1] * page
    lam_init = 0.8 - 0.6 * math.exp(-0.3 * 0)
    tp, ts = batch * seq, bd

    w_in_b = w_in.reshape(w_in.shape[1:]).astype(BF16)
    w_out_b = w_out.reshape(w_out.shape[1:]).astype(BF16)
    w_ck_b, w_cv_b, w_cq_b, w_co_b = (w.reshape(w.shape[1:]).astype(BF16) for w in (w_ck, w_cv, w_cq, w_co))
    lams = tuple(v.reshape(1, QK_DIM) for v in (lambda_q1[0], lambda_k1[0], lambda_q2[0], lambda_k2[0]))
    gsub = g_subln[0].reshape(1, HEAD_DIM)
    attn_scale = QK_DIM ** -0.5
    cross_scale = (cross_w // N_CROSS_HEADS) ** -0.5
    col_q, col_k, col_v = 0, aw, 2 * aw
    col_b, col_c, col_x = 3 * aw, 3 * aw + cw, 3 * aw + 2 * cw

    def in_proj(a, rope, period):
        q = _matmul([a], w_in_b, col_q, aw, out_dtype=BF16, mode="rope", scale=attn_scale, rope=rope,
                    rope_period=period, name="in_proj_q")
        k = _matmul([a], w_in_b, col_k, aw, out_dtype=F32, mode="rope", rope=rope, rope_period=period,
                    name="in_proj_k")
        v = _matmul([a], w_in_b, col_v, aw, out_dtype=F32, name="in_proj_v")
        return q, k, v

    def after_mixer(x2d, o_attn, g_conv, mem_k, mem_v, cross_fn):
        h1 = _matmul([o_attn, g_conv], w_out_b, 0, d, out_dtype=F32, mode="residual", residual=x2d,
                     bn_pref=512, name="out_proj")
        a2 = _rmsnorm_bf16(h1, g_cross[0])
        qc = _matmul([a2], w_cq_b, 0, cross_w, out_dtype=BF16, scale=cross_scale, name="cross_q")
        oc = cross_fn(qc, mem_k, mem_v)
        return _matmul([oc], w_co_b, 0, d, out_dtype=F32, mode="residual", residual=h1, bn_pref=512,
                       name="cross_out")

    xp = x_prompt.reshape(tp, d)
    a_p = _rmsnorm_bf16(xp, g_mix[0])
    rope_p = _rope_tables(jnp.arange(seq, dtype=jnp.int32))
    q_p, k_p, v_p = in_proj(a_p, rope_p, seq)
    o_p = _causal_diff_attn(q_p.reshape(batch, seq, aw), k_p.reshape(batch, seq, aw),
                            v_p.reshape(batch, seq, aw), lams, gsub, lam_init).reshape(tp, aw)
    g_p, conv_p = _conv_branch_seq(a_p, w_in_b, col_b, col_c, col_x, w_conv[0], batch, seq)
    m_p = _rmsnorm_bf16(mem_prompt.reshape(batch * n_mem, d), g_mem[0])
    mem_k_p = _matmul([m_p], w_ck_b, 0, cross_w, out_dtype=F32, name="mem_k")
    mem_v_p = _matmul([m_p], w_cv_b, 0, cross_w, out_dtype=F32, name="mem_v")
    h2_p = after_mixer(
        xp, o_p, g_p, mem_k_p.reshape(batch, n_mem, cross_w), mem_v_p.reshape(batch, n_mem, cross_w),
        lambda qc, mk, mv: _cross_attn_seq(qc.reshape(batch, seq, cross_w), mk, mv).reshape(tp, cross_w))

    xs = x_sample.reshape(ts, d)
    a_s = _rmsnorm_bf16(xs, g_mix[0])
    rope_s = _rope_tables(jnp.full((ts,), past_len, dtype=jnp.int32))
    q_s, k_s, v_s = in_proj(a_s, rope_s, ts)
    hd3 = (ts, n_heads, HEAD_DIM)
    o_s = _paged_diff_attn(q_s.reshape(hd3), k_s.reshape(hd3), v_s.reshape(hd3),
                           cache_k.reshape(n_pool, page, n_heads, HEAD_DIM),
                           cache_v.reshape(n_pool, page, n_heads, HEAD_DIM),
                           page_table, lams, gsub, lam_init).reshape(ts, aw)
    st = state_conv[0]
    g_s, z_s = _conv_branch_step(a_s, w_in_b, col_b, col_c, col_x, w_conv[0], st[:, 0, :], st[:, 1, :])
    conv_s = jnp.stack([st[:, 1, :], z_s], axis=1)
    h2_s = after_mixer(xs, o_s, g_s, cache_mem_k.reshape(cache_mem_k.shape[1:]),
                       cache_mem_v.reshape(cache_mem_v.shape[1:]), _cross_attn_step)

    t_all = tp + ts
    w_r = jnp.concatenate([w_router_group[0], w_router_expert[0],
                           jnp.zeros((d, LANES - N_GROUPS - N_EXPERTS), F32)], axis=1)
    b_r = jnp.concatenate([b_router_group[0], b_router_expert[0],
                           jnp.zeros((LANES - N_GROUPS - N_EXPERTS,), F32)]).reshape(1, LANES)
    g_ffn2 = g_ffn[0].reshape(1, d)
    meta_p = _router(h2_p, g_ffn2, w_r, b_r)
    meta_s = _router(h2_s, g_ffn2, w_r, b_r)
    n_units = (2 * t_all + N_EXPERTS * (MOE_ROWS - 1) + MOE_ROWS - 1) // MOE_ROWS
    unit_expert, unit_rows, buf_tok, nused, slots = _dispatch_plan(
        jnp.concatenate([meta_p[:, 0:2], meta_s[:, 0:2]], axis=0), n_units)
    expert_out = _experts(h2_p, h2_s, g_ffn2, w_e_gate.reshape(w_e_gate.shape[1:]),
                          w_e_up.reshape(w_e_up.shape[1:]), w_e_down.reshape(w_e_down.shape[1:]),
                          unit_expert, unit_rows, buf_tok, nused, n_units)
    g_fin = g_final.reshape(1, d)
    y_p = _combine(h2_p, meta_p, g_fin, expert_out, slots, 0)
    y_s = _combine(h2_s, meta_s, g_fin, expert_out, slots, tp)

    return (y_p.reshape(batch, seq, d), y_s.reshape(bd, 1, d),
            k_p.reshape(1, batch, seq, n_heads, HEAD_DIM), v_p.reshape(1, batch, seq, n_heads, HEAD_DIM),
            conv_p.reshape(1, batch, CONV_K - 1, cw),
            mem_k_p.reshape(1, batch, n_mem, N_CROSS_HEADS, cross_w // N_CROSS_HEADS),
            mem_v_p.reshape(1, batch, n_mem, N_CROSS_HEADS, cross_w // N_CROSS_HEADS),
            k_s.reshape(1, bd, 1, n_heads, HEAD_DIM), v_s.reshape(1, bd, 1, n_heads, HEAD_DIM),
            conv_s.reshape(1, bd, CONV_K - 1, cw))
```
